```python
import math, functools
import jax, jax.numpy as jnp
from jax import lax
import numpy as np

D_MODEL = 1024
BATCH = 2
SEQ = 16384
DEPTH = 1
DEC_BATCH = 16
DEC_SEQ = 16
PAST_LEN = 2048

CHUNK = 64
Q_BLOCK = 128
A_HEADS = 8
A_HEAD_DIM = 64
A_WIDTH = A_HEADS * A_HEAD_DIM
M_HEADS = 4
M_HEAD_DIM = 128
M_WIDTH = M_HEADS * M_HEAD_DIM
CONV_W = 4
D_FF = 2816
LN_EPS = 1e-5
ALPHA = (2.0 * DEPTH) ** 0.25
BETA = (8.0 * DEPTH) ** -0.25
IN_SIZES = (A_WIDTH, A_WIDTH, A_WIDTH, A_HEADS,
            M_WIDTH, M_WIDTH, M_WIDTH, M_HEADS, M_HEADS, M_WIDTH,
            D_MODEL, D_MODEL)
D_IN = sum(IN_SIZES)

kernel_name = "fox_mlstm_gated_macaron_deepnorm_adaln_step"


def _layer_norm(x, g, b):
    xf = x.astype(jnp.float32)
    mu = jnp.mean(xf, axis=-1, keepdims=True)
    var = jnp.mean(jnp.square(xf - mu), axis=-1, keepdims=True)
    return ((xf - mu) * lax.rsqrt(var + LN_EPS) * g + b).astype(x.dtype)


def _ffn(h, w_gu, w_down):
    g, u = jnp.split(h @ w_gu, 2, axis=-1)
    return (jax.nn.silu(g) * u) @ w_down


def _split_in(z):
    idx = [int(i) for i in np.cumsum(IN_SIZES)[:-1]]
    return jnp.split(z, idx, axis=-1)


def _causal_conv(u, buf, w, b):
    L = u.shape[1]
    full = jnp.concatenate([buf.astype(u.dtype), u], axis=1)
    out = b
    for j in range(CONV_W):
        out = out + full[:, j:j + L] * w[j]
    return out, full[:, -(CONV_W - 1):]


def _branch_inputs(h, w_in, b_in, conv_w, conv_b, conv_buf):
    B, L = h.shape[:2]
    z = h @ w_in + b_in
    aq, ak, av, af, mq, mk, mv, mi, mf, mo, ga, gb = _split_in(z)
    qk, new_buf = _causal_conv(jnp.concatenate([mq, mk], axis=-1), conv_buf, conv_w, conv_b)
    mq, mk = jnp.split(jax.nn.silu(qk), 2, axis=-1)
    fox = (aq.reshape(B, L, A_HEADS, A_HEAD_DIM),
           ak.reshape(B, L, A_HEADS, A_HEAD_DIM),
           av.reshape(B, L, A_HEADS, A_HEAD_DIM),
           jax.nn.log_sigmoid(af.astype(jnp.float32)))
    f32 = jnp.float32
    mls = (mq.reshape(B, L, M_HEADS, M_HEAD_DIM).astype(f32),
           mk.reshape(B, L, M_HEADS, M_HEAD_DIM).astype(f32) * (M_HEAD_DIM ** -0.5),
           mv.reshape(B, L, M_HEADS, M_HEAD_DIM).astype(f32),
           mi.astype(f32),
           jax.nn.log_sigmoid(mf.astype(f32)),
           mo)
    return fox, mls, ga, gb, new_buf


def _fox_prompt(q, k, v, logf):
    B, S = q.shape[:2]
    nb = S // Q_BLOCK
    Ft = jnp.cumsum(logf, axis=1).transpose(0, 2, 1)
    qb = q.reshape(B, nb, Q_BLOCK, A_HEADS, A_HEAD_DIM).transpose(1, 0, 2, 3, 4)
    Fb = Ft.reshape(B, A_HEADS, nb, Q_BLOCK).transpose(2, 0, 1, 3)
    pos_k = jnp.arange(S)
    scale = A_HEAD_DIM ** -0.5

    def block(args):
        i, qi, Fi = args
        s = jnp.einsum('bqhd,bkhd->bhqk', qi, k, preferred_element_type=jnp.float32) * scale
        s = s + Fi[..., :, None] - Ft[..., None, :]
        pos_q = i * Q_BLOCK + jnp.arange(Q_BLOCK)
        s = jnp.where(pos_k[None, :] <= pos_q[:, None], s, -jnp.inf)
        p = jax.nn.softmax(s, axis=-1)
        return jnp.einsum('bhqk,bkhd->bqhd', p.astype(v.dtype), v)

    out = lax.map(block, (jnp.arange(nb), qb, Fb))
    return out.transpose(1, 0, 2, 3, 4).reshape(B, S, A_WIDTH)


def _fox_sample(q, k_new, v_new, logf_new, k_cache, v_cache, logf_cache):
    B, L = q.shape[:2]
    P = k_cache.shape[1]
    k = jnp.concatenate([k_cache.astype(k_new.dtype), k_new], axis=1)
    v = jnp.concatenate([v_cache.astype(v_new.dtype), v_new], axis=1)
    logf = jnp.concatenate([logf_cache.astype(jnp.float32), logf_new], axis=1)
    Ft = jnp.cumsum(logf, axis=1).transpose(0, 2, 1)
    s = jnp.einsum('bqhd,bkhd->bhqk', q, k, preferred_element_type=jnp.float32) * (A_HEAD_DIM ** -0.5)
    s = s + Ft[..., P:, None] - Ft[..., None, :]
    pos_q = P + jnp.arange(L)
    pos_k = jnp.arange(P + L)
    s = jnp.where(pos_k[None, :] <= pos_q[:, None], s, -jnp.inf)
    p = jax.nn.softmax(s, axis=-1)
    return jnp.einsum('bhqk,bkhd->bqhd', p.astype(v.dtype), v).reshape(B, L, A_WIDTH)


def _mlstm_chunk(carry, inp):
    C, n, m = carry
    q, k, v, ig, lf = inp
    L = q.shape[1]
    b = jnp.cumsum(lf, axis=1).transpose(0, 2, 1)
    it = ig.transpose(0, 2, 1)
    causal = jnp.tril(jnp.ones((L, L), dtype=bool))
    d = jnp.where(causal, b[..., :, None] - b[..., None, :] + it[..., None, :], -jnp.inf)
    inter = b + m[..., None]
    m_t = jnp.maximum(inter, jnp.max(d, axis=-1))
    w_inter = jnp.exp(inter - m_t)
    a = jnp.exp(d - m_t[..., None]) * jnp.einsum('blhd,bshd->bhls', q, k)
    num = jnp.einsum('bhls,bshv->bhlv', a, v) + w_inter[..., None] * jnp.einsum('bhvd,blhd->bhlv', C, q)
    den = jnp.sum(a, axis=-1) + w_inter * jnp.einsum('bhd,blhd->bhl', n, q)
    h = num / jnp.maximum(jnp.abs(den), jnp.exp(-m_t))[..., None]
    m_new = m_t[..., -1]
    w_state = jnp.exp(b[..., -1] + m - m_new)
    w_s = jnp.exp(b[..., -1:] - b + it - m_new[..., None])
    C_new = w_state[..., None, None] * C + jnp.einsum('bhs,bshv,bshd->bhvd', w_s, v, k)
    n_new = w_state[..., None] * n + jnp.einsum('bhs,bshd->bhd', w_s, k)
    return (C_new, n_new, m_new), h.transpose(0, 2, 1, 3)


def _mlstm_prompt(q, k, v, ig, lf):
    B, S = q.shape[:2]
    nc = S // CHUNK

    def to_chunks(a):
        return a.reshape((B, nc, CHUNK) + a.shape[2:]).swapaxes(0, 1)

    init = (jnp.zeros((B, M_HEADS, M_HEAD_DIM, M_HEAD_DIM), jnp.float32),
            jnp.zeros((B, M_HEADS, M_HEAD_DIM), jnp.float32),
            jnp.zeros((B, M_HEADS), jnp.float32))
    carry, hs = lax.scan(_mlstm_chunk, init, tuple(to_chunks(a) for a in (q, k, v, ig, lf)))
    return hs.swapaxes(0, 1).reshape(B, S, M_HEADS, M_HEAD_DIM), carry


def _mlstm_out(h, o, g):
    B, L = h.shape[:2]
    mu = jnp.mean(h, axis=-1, keepdims=True)
    var = jnp.mean(jnp.square(h - mu), axis=-1, keepdims=True)
    hn = ((h - mu) * lax.rsqrt(var + LN_EPS)).reshape(B, L, M_WIDTH)
    return (hn * g * jax.nn.sigmoid(o.astype(jnp.float32))).astype(o.dtype)


def _merge(o_a, o_b, ga, gb, w_branch_a, w_branch_b, w_out):
    m = jax.nn.sigmoid(ga) * (o_a @ w_branch_a) + jax.nn.sigmoid(gb) * (o_b @ w_branch_b)
    return m @ w_out


def _mixer_prompt(h, w_in, b_in, conv_w, conv_b, norm_g, w_branch_a, w_branch_b, w_out):
    B = h.shape[0]
    buf0 = jnp.zeros((B, CONV_W - 1, 2 * M_WIDTH), h.dtype)
    fox, mls, ga, gb, new_buf = _branch_inputs(h, w_in, b_in, conv_w, conv_b, buf0)
    aq, ak, av, alf = fox
    o_a = _fox_prompt(aq, ak, av, alf)
    mq, mk, mv, mi, mf, mo = mls
    hm, (C, n, m) = _mlstm_prompt(mq, mk, mv, mi, mf)
    o_b = _mlstm_out(hm, mo, norm_g)
    y = _merge(o_a, o_b, ga, gb, w_branch_a, w_branch_b, w_out)
    return y, (ak, av, alf, C, n, m, new_buf)


def _mixer_sample(h, k_cache, v_cache, logf_cache, C0, n0, m0, conv_buf,
                  w_in, b_in, conv_w, conv_b, norm_g, w_branch_a, w_branch_b, w_out):
    fox, mls, ga, gb, new_buf = _branch_inputs(h, w_in, b_in, conv_w, conv_b, conv_buf)
    aq, ak, av, alf = fox
    o_a = _fox_sample(aq, ak, av, alf, k_cache, v_cache, logf_cache)
    mq, mk, mv, mi, mf, mo = mls
    carry0 = (C0.astype(jnp.float32), n0.astype(jnp.float32), m0.astype(jnp.float32))
    (C, n, m), hm = _mlstm_chunk(carry0, (mq, mk, mv, mi, mf))
    o_b = _mlstm_out(hm, mo, norm_g)
    y = _merge(o_a, o_b, ga, gb, w_branch_a, w_branch_b, w_out)
    return y, (ak, av, alf, C, n, m, new_buf)


def _trunk_layer(x, c, mixer_fn, w_ada, b_ada, ffn1_w_gu, ffn1_w_down, ffn2_w_gu, ffn2_w_down, ln_g, ln_b):
    mod = jax.nn.silu(c) @ w_ada + b_ada
    sh1, sc1, g1, sh2, sc2, g2, sh3, sc3, g3 = jnp.split(mod[:, None, :], 9, axis=-1)
    h = x * (1 + sc1) + sh1
    x = _layer_norm(ALPHA * x + 0.5 * g1 * _ffn(h, ffn1_w_gu, ffn1_w_down), ln_g[0], ln_b[0])
    h = x * (1 + sc2) + sh2
    mix, new_state = mixer_fn(h)
    x = _layer_norm(ALPHA * x + g2 * mix, ln_g[1], ln_b[1])
    h = x * (1 + sc3) + sh3
    x = _layer_norm(ALPHA * x + 0.5 * g3 * _ffn(h, ffn2_w_gu, ffn2_w_down), ln_g[2], ln_b[2])
    return x, new_state


def setup_inputs(seed: int = 0) -> dict:
    key = jax.random.key(seed)
    ks = jax.random.split(key, 32)
    nrm = jax.random.normal
    f32 = jnp.float32
    D = D_MODEL
    off = [int(o) for o in np.cumsum((0,) + IN_SIZES)]
    b_in = 0.02 * nrm(ks[0], (DEPTH, D_IN), f32)
    b_in = b_in.at[:, off[3]:off[4]].add(jnp.linspace(1.0, 4.0, A_HEADS))
    b_in = b_in.at[:, off[8]:off[9]].add(jnp.linspace(3.0, 6.0, M_HEADS))
    return {
        "x_prompt": nrm(ks[1], (BATCH, SEQ, D), f32),
        "x_sample": nrm(ks[2], (DEC_BATCH, DEC_SEQ, D), f32),
        "cache_fox_k": nrm(ks[3], (DEPTH, DEC_BATCH, PAST_LEN, A_HEADS, A_HEAD_DIM), f32),
        "cache_fox_v": nrm(ks[4], (DEPTH, DEC_BATCH, PAST_LEN, A_HEADS, A_HEAD_DIM), f32),
        "cache_fox_logf": jax.nn.log_sigmoid(3.0 + nrm(ks[5], (DEPTH, DEC_BATCH, PAST_LEN, A_HEADS), f32)),
        "state_mlstm_C": 0.1 * nrm(ks[6], (DEPTH, DEC_BATCH, M_HEADS, M_HEAD_DIM, M_HEAD_DIM), f32),
        "state_mlstm_n": 0.1 * nrm(ks[7], (DEPTH, DEC_BATCH, M_HEADS, M_HEAD_DIM), f32),
        "state_mlstm_m": nrm(ks[8], (DEPTH, DEC_BATCH, M_HEADS), f32),
        "state_conv": nrm(ks[9], (DEPTH, DEC_BATCH, CONV_W - 1, 2 * M_WIDTH), f32),
        "c_prompt": nrm(ks[10], (BATCH, D), f32),
        "c_sample": nrm(ks[11], (DEC_BATCH, D), f32),
        "w_ada": 0.5 * D ** -0.5 * nrm(ks[12], (DEPTH, D, 9 * D), f32),
        "b_ada": 0.02 * nrm(ks[13], (DEPTH, 9 * D), f32),
        "ffn1_w_gu": D ** -0.5 * nrm(ks[14], (DEPTH, D, 2 * D_FF), f32),
        "ffn1_w_down": BETA * D_FF ** -0.5 * nrm(ks[15], (DEPTH, D_FF, D), f32),
        "w_in": D ** -0.5 * nrm(ks[16], (DEPTH, D, D_IN), f32),
        "b_in": b_in,
        "conv_w": CONV_W ** -0.5 * nrm(ks[17], (DEPTH, CONV_W, 2 * M_WIDTH), f32),
        "conv_b": 0.02 * nrm(ks[18], (DEPTH, 2 * M_WIDTH), f32),
        "mlstm_norm_g": 1.0 + 0.02 * nrm(ks[19], (DEPTH, M_WIDTH), f32),
        "w_branch_a": A_WIDTH ** -0.5 * nrm(ks[20], (DEPTH, A_WIDTH, D), f32),
        "w_branch_b": M_WIDTH ** -0.5 * nrm(ks[21], (DEPTH, M_WIDTH, D), f32),
        "w_out": BETA * D ** -0.5 * nrm(ks[22], (DEPTH, D, D), f32),
        "ffn2_w_gu": D ** -0.5 * nrm(ks[23], (DEPTH, D, 2 * D_FF), f32),
        "ffn2_w_down": BETA * D_FF ** -0.5 * nrm(ks[24], (DEPTH, D_FF, D), f32),
        "ln_g": 1.0 + 0.02 * nrm(ks[25], (DEPTH, 3, D), f32),
        "ln_b": 0.02 * nrm(ks[26], (DEPTH, 3, D), f32),
    }


def reference(x_prompt, x_sample, cache_fox_k, cache_fox_v, cache_fox_logf, state_mlstm_C,
              state_mlstm_n, state_mlstm_m, state_conv, c_prompt, c_sample, w_ada, b_ada,
              ffn1_w_gu, ffn1_w_down, w_in, b_in, conv_w, conv_b, mlstm_norm_g, w_branch_a,
              w_branch_b, w_out, ffn2_w_gu, ffn2_w_down, ln_g, ln_b):
    xp, xs = x_prompt, x_sample
    sp = [[] for _ in range(7)]
    ss = [[] for _ in range(7)]
    for l in range(DEPTH):
        mix_w = (w_in[l], b_in[l], conv_w[l], conv_b[l], mlstm_norm_g[l], w_branch_a[l], w_branch_b[l], w_out[l])
        layer_w = (w_ada[l], b_ada[l], ffn1_w_gu[l], ffn1_w_down[l], ffn2_w_gu[l], ffn2_w_down[l], ln_g[l], ln_b[l])
        xp, st_p = _trunk_layer(xp, c_prompt, lambda h: _mixer_prompt(h, *mix_w), *layer_w)
        xs, st_s = _trunk_layer(
            xs, c_sample,
            lambda h: _mixer_sample(h, cache_fox_k[l], cache_fox_v[l], cache_fox_logf[l], state_mlstm_C[l],
                                    state_mlstm_n[l], state_mlstm_m[l], state_conv[l], *mix_w),
            *layer_w)
        for j in range(7):
            sp[j].append(st_p[j])
            ss[j].append(st_s[j])
    fox_k_p, fox_v_p, fox_logf_p, mlstm_C_p, mlstm_n_p, mlstm_m_p, conv_p = [jnp.stack(a, 0) for a in sp]
    fox_k_s, fox_v_s, fox_logf_s, mlstm_C_s, mlstm_n_s, mlstm_m_s, conv_s = [jnp.stack(a, 0) for a in ss]
    return (xp, xs, fox_k_p, fox_v_p, fox_logf_p, mlstm_C_p, mlstm_n_p, mlstm_m_p, conv_p,
            fox_k_s, fox_v_s, fox_logf_s, mlstm_C_s, mlstm_n_s, mlstm_m_s, conv_s)
```

```python
import functools

import jax
import jax.numpy as jnp
import numpy as np
from jax import lax
from jax.experimental import pallas as pl
from jax.experimental.pallas import tpu as pltpu

F32 = jnp.float32
BF16 = jnp.bfloat16

D_MODEL = 1024
A_HEADS = 8
A_HEAD_DIM = 64
A_WIDTH = A_HEADS * A_HEAD_DIM
M_HEADS = 4
M_HEAD_DIM = 128
M_WIDTH = M_HEADS * M_HEAD_DIM
CONV_W = 4
D_FF = 2816
LN_EPS = 1e-5
DEPTH = 1
ALPHA = (2.0 * DEPTH) ** 0.25
LANES = 128
NEG_BIG = -1e30
VMEM_LIMIT = 56 * 1024 * 1024

MAIN_COLS = 3 * A_WIDTH + 4 * M_WIDTH + 2 * D_MODEL
G_IG = A_HEADS
G_LF = A_HEADS + M_HEADS


def _params(sem):
    return pltpu.CompilerParams(dimension_semantics=sem, vmem_limit_bytes=VMEM_LIMIT)


def _resident(shape):
    nd = len(shape)
    return pl.BlockSpec(shape, lambda *_: (0,) * nd, pipeline_mode=pl.Buffered(1))


def _sigmoid(x):
    return 1.0 / (1.0 + jnp.exp(-x))


def _layer_norm(y, g, b):
    mu = jnp.mean(y, axis=-1, keepdims=True)
    yc = y - mu
    var = jnp.mean(yc * yc, axis=-1, keepdims=True)
    return yc * lax.rsqrt(var + LN_EPS) * g + b


def _dot(a, b):
    return jnp.dot(a, b, preferred_element_type=F32)


def _dot_nt(a, b):
    return lax.dot_general(a, b, (((1,), (1,)), ((), ())), preferred_element_type=F32)


def _dot_tn(a, b):
    return lax.dot_general(a, b, (((0,), (0,)), ((), ())), preferred_element_type=F32)


def _dot_exact(a, b):
    return jnp.dot(a, b, preferred_element_type=F32, precision=lax.Precision.HIGHEST)


def _ada_kernel(c_ref, w_ref, b_ref, o_ref):
    c = c_ref[...]
    o_ref[...] = _dot(c * _sigmoid(c), w_ref[...]) + b_ref[...]


def _ada(c_all, w_ada, b_ada):
    rows = c_all.shape[0]
    n = w_ada.shape[1]
    tn = 1024
    return pl.pallas_call(
        _ada_kernel,
        grid=(n // tn,),
        in_specs=[pl.BlockSpec((rows, D_MODEL), lambda j: (0, 0)),
                  pl.BlockSpec((D_MODEL, tn), lambda j: (0, j)),
                  pl.BlockSpec((1, tn), lambda j: (0, j))],
        out_specs=pl.BlockSpec((rows, tn), lambda j: (0, j)),
        out_shape=jax.ShapeDtypeStruct((rows, n), F32),
        compiler_params=_params(("parallel",)),
        name="ada",
    )(c_all, w_ada, b_ada)


FFN_CHUNK = 256


def _ffn_kernel(x_ref, sh_ref, sc_ref, g_ref, wgu_ref, wd_ref, lng_ref, lnb_ref, o_ref):
    x = x_ref[...]
    h = (x * (1.0 + sc_ref[0]) + sh_ref[0]).astype(BF16)
    acc = jnp.zeros(x.shape, F32)
    for c in range(D_FF // FFN_CHUNK):
        lo = c * FFN_CHUNK
        gate = _dot(h, wgu_ref[:, lo:lo + FFN_CHUNK])
        up = _dot(h, wgu_ref[:, D_FF + lo:D_FF + lo + FFN_CHUNK])
        act = (gate * _sigmoid(gate) * up).astype(BF16)
        acc = acc + _dot(act, wd_ref[lo:lo + FFN_CHUNK, :])
    y = ALPHA * x + 0.5 * g_ref[0] * acc
    o_ref[...] = _layer_norm(y, lng_ref[...], lnb_ref[...])


def _mod_spec(mod, group, tiles_per_batch):
    return pl.BlockSpec((1, mod.shape[1], D_MODEL), lambda i: (i // tiles_per_batch, 0, group))


def _ffn(x, mod, group0, tiles_per_batch, tm, w_gu, w_down, ln_g, ln_b):
    rows = x.shape[0]
    row_spec = pl.BlockSpec((tm, D_MODEL), lambda i: (i, 0))
    return pl.pallas_call(
        _ffn_kernel,
        grid=(rows // tm,),
        in_specs=[row_spec,
                  _mod_spec(mod, group0, tiles_per_batch),
                  _mod_spec(mod, group0 + 1, tiles_per_batch),
                  _mod_spec(mod, group0 + 2, tiles_per_batch),
                  _resident(w_gu.shape), _resident(w_down.shape),
                  _resident(ln_g.shape), _resident(ln_b.shape)],
        out_specs=row_spec,
        out_shape=jax.ShapeDtypeStruct((rows, D_MODEL), F32),
        compiler_params=_params(("parallel",)),
        name="ffn",
    )(x, mod, mod, mod, w_gu, w_down, ln_g, ln_b)


def _inproj_kernel(x_ref, sh_ref, sc_ref, wm_ref, bm_ref, ws_ref, bs_ref,
                   q_ref, kf_ref, vf_ref, kb_ref, vb_ref, gat_ref, u_ref, mv_ref, mo_ref, gab_ref):
    x = x_ref[...]
    h = (x * (1.0 + sc_ref[0]) + sh_ref[0]).astype(BF16)

    def proj(lo, hi):
        return _dot(h, wm_ref[:, lo:hi]) + bm_ref[:, lo:hi]

    aw, mw = A_WIDTH, M_WIDTH
    q_ref[...] = (proj(0, aw) * (A_HEAD_DIM ** -0.5)).astype(BF16)
    k = proj(aw, 2 * aw)
    kf_ref[...] = k
    kb_ref[...] = k.astype(BF16)
    v = proj(2 * aw, 3 * aw)
    vf_ref[...] = v
    vb_ref[...] = v.astype(BF16)
    o = 3 * aw
    u_ref[...] = proj(o, o + 2 * mw)
    mv_ref[...] = proj(o + 2 * mw, o + 3 * mw).astype(BF16)
    mo_ref[...] = proj(o + 3 * mw, o + 4 * mw)
    gab_ref[...] = proj(o + 4 * mw, o + 4 * mw + 2 * D_MODEL)

    z = _dot(h, ws_ref[...]) + bs_ref[...]
    log_sig = jnp.minimum(z, 0.0) - jnp.log1p(jnp.exp(-jnp.abs(z)))
    lane = lax.broadcasted_iota(jnp.int32, z.shape, 1)
    is_input_gate = (lane >= G_IG) & (lane < G_LF)
    gat_ref[...] = jnp.where(is_input_gate, z, log_sig)


def _inproj(x, mod, tiles_per_batch, tm, w_main, b_main, w_small, b_small):
    rows = x.shape[0]

    def rs(width):
        return pl.BlockSpec((tm, width), lambda i: (i, 0))

    def out(width, dt):
        return jax.ShapeDtypeStruct((rows, width), dt)

    return pl.pallas_call(
        _inproj_kernel,
        grid=(rows // tm,),
        in_specs=[rs(D_MODEL),
                  _mod_spec(mod, 3, tiles_per_batch),
                  _mod_spec(mod, 4, tiles_per_batch),
                  _resident(w_main.shape), _resident(b_main.shape),
                  _resident(w_small.shape), _resident(b_small.shape)],
        out_specs=[rs(A_WIDTH), rs(A_WIDTH), rs(A_WIDTH), rs(A_WIDTH), rs(A_WIDTH),
                   rs(LANES), rs(2 * M_WIDTH), rs(M_WIDTH), rs(M_WIDTH), rs(2 * D_MODEL)],
        out_shape=[out(A_WIDTH, BF16), out(A_WIDTH, F32), out(A_WIDTH, F32),
                   out(A_WIDTH, BF16), out(A_WIDTH, BF16), out(LANES, F32),
                   out(2 * M_WIDTH, F32), out(M_WIDTH, BF16), out(M_WIDTH, F32),
                   out(2 * D_MODEL, F32)],
        compiler_params=_params(("parallel",)),
        name="inproj",
    )(x, mod, mod, w_main, b_main, w_small, b_small)


def _cumsum_kernel(x_ref, o_ref, carry_ref):
    @pl.when(pl.program_id(0) == 0)
    def _():
        carry_ref[...] = jnp.zeros_like(carry_ref)

    t = x_ref.shape[1]
    upper = (lax.broadcasted_iota(jnp.int32, (t, t), 0)
             <= lax.broadcasted_iota(jnp.int32, (t, t), 1)).astype(F32)
    out = _dot_exact(x_ref[...], upper) + carry_ref[...]
    o_ref[...] = out
    carry_ref[...] = out[:, t - 1:t]


def _cumsum_lanes(x, block):
    rows, total = x.shape
    spec = pl.BlockSpec((rows, block), lambda j: (0, j))
    return pl.pallas_call(
        _cumsum_kernel,
        grid=(total // block,),
        in_specs=[spec],
        out_specs=spec,
        out_shape=jax.ShapeDtypeStruct((rows, total), F32),
        scratch_shapes=[pltpu.VMEM((rows, 1), F32)],
        compiler_params=_params(("arbitrary",)),
        name="cumsum",
    )(x)


HALF = LANES // 2


def _head_masks(q):
    lane = lax.broadcasted_iota(jnp.int32, q.shape, 1)
    zero = jnp.zeros_like(q)
    return jnp.where(lane < HALF, q, zero), jnp.where(lane >= HALF, q, zero)


def _fox_prompt_kernel(q_ref, k_ref, v_ref, f_ref, o_ref, m_ref, l_ref, acc_ref, *, blk):
    qi = pl.program_id(2)
    q_heads = _head_masks(q_ref[0])
    m_ref[...] = jnp.full(m_ref.shape, NEG_BIG, F32)
    l_ref[...] = jnp.zeros(l_ref.shape, F32)
    acc_ref[...] = jnp.zeros(acc_ref.shape, F32)

    def step(kj, on_diagonal):
        start = pl.multiple_of(kj * blk, blk)
        k = k_ref[0, pl.ds(start, blk), :]
        v = v_ref[0, pl.ds(start, blk), :]
        for hh in range(2):
            s = _dot_nt(q_heads[hh], k) - f_ref[0, 0, hh:hh + 1, pl.ds(start, blk)]
            if on_diagonal:
                row = lax.broadcasted_iota(jnp.int32, s.shape, 0)
                col = lax.broadcasted_iota(jnp.int32, s.shape, 1)
                s = jnp.where(col <= row, s, NEG_BIG)
            m_old = m_ref[hh]
            m_new = jnp.maximum(m_old, jnp.max(s, axis=-1, keepdims=True))
            alpha = jnp.exp(m_old - m_new)
            p = jnp.exp(s - m_new)
            l_ref[hh] = alpha * l_ref[hh] + jnp.sum(p, axis=-1, keepdims=True)
            acc_ref[hh] = alpha * acc_ref[hh] + _dot(p.astype(BF16), v)
            m_ref[hh] = m_new

    def body(kj, carry):
        step(kj, False)
        return carry

    lax.fori_loop(0, qi, body, 0)
    step(qi, True)
    lane = lax.broadcasted_iota(jnp.int32, (blk, LANES), 1)
    o = jnp.where(lane < HALF, acc_ref[0] / l_ref[0], acc_ref[1] / l_ref[1])
    o_ref[0] = o.astype(o_ref.dtype)


def _fox_prompt(q, k, v, f_rows, blk):
    b, s, _ = q.shape
    pairs = A_HEADS // 2
    q_spec = pl.BlockSpec((1, blk, LANES), lambda bi, p, i: (bi, i, p))
    kv_spec = pl.BlockSpec((1, s, LANES), lambda bi, p, i: (bi, 0, p))
    return pl.pallas_call(
        functools.partial(_fox_prompt_kernel, blk=blk),
        grid=(b, pairs, s // blk),
        in_specs=[q_spec, kv_spec, kv_spec,
                  pl.BlockSpec((1, 1, 2, s), lambda bi, p, i: (bi, p, 0, 0))],
        out_specs=q_spec,
        out_shape=jax.ShapeDtypeStruct((b, s, A_WIDTH), BF16),
        scratch_shapes=[pltpu.VMEM((2, blk, 1), F32), pltpu.VMEM((2, blk, 1), F32),
                        pltpu.VMEM((2, blk, LANES), F32)],
        compiler_params=_params(("parallel", "parallel", "arbitrary")),
        name="fox_prompt",
    )(q, k, v, f_rows)


def _fox_sample_kernel(q_ref, kc_ref, vc_ref, kn_ref, vn_ref, gc_ref, gn_ref, o_ref, *, n_new):
    rows = q_ref.shape[1]
    pad = kn_ref.shape[1]
    row = lax.broadcasted_iota(jnp.int32, (rows, pad), 0)
    col = lax.broadcasted_iota(jnp.int32, (rows, pad), 1)
    visible = (col <= row) & (col < n_new)
    lane = lax.broadcasted_iota(jnp.int32, (rows, LANES), 1)
    for p in range(A_HEADS // 2):
        cols = slice(p * LANES, (p + 1) * LANES)
        q_heads = _head_masks(q_ref[0, :, cols])
        kc = kc_ref[0, :, cols].astype(BF16)
        vc = vc_ref[0, :, cols].astype(BF16)
        kn = kn_ref[0, :, cols]
        vn = vn_ref[0, :, cols]
        outs = []
        for hh in range(2):
            h = 2 * p + hh
            s_c = _dot_nt(q_heads[hh], kc) - gc_ref[0, h:h + 1, :]
            s_n = jnp.where(visible, _dot_nt(q_heads[hh], kn) - gn_ref[0, h:h + 1, :], NEG_BIG)
            m = jnp.maximum(jnp.max(s_c, axis=-1, keepdims=True), jnp.max(s_n, axis=-1, keepdims=True))
            p_c = jnp.exp(s_c - m)
            p_n = jnp.exp(s_n - m)
            l = jnp.sum(p_c, axis=-1, keepdims=True) + jnp.sum(p_n, axis=-1, keepdims=True)
            outs.append((_dot(p_c.astype(BF16), vc) + _dot(p_n.astype(BF16), vn)) / l)
        o_ref[0, :, cols] = jnp.where(lane < HALF, outs[0], outs[1]).astype(o_ref.dtype)


def _fox_sample(q, k_cache, v_cache, k_new, v_new, g_cache, g_new, n_new):
    b, l, w = q.shape
    p = k_cache.shape[1]
    pad = k_new.shape[1]

    def spec(r, c):
        return pl.BlockSpec((1, r, c), lambda i: (i, 0, 0))

    return pl.pallas_call(
        functools.partial(_fox_sample_kernel, n_new=n_new),
        grid=(b,),
        in_specs=[spec(l, w), spec(p, w), spec(p, w), spec(pad, w), spec(pad, w),
                  spec(A_HEADS, p), spec(A_HEADS, pad)],
        out_specs=spec(l, w),
        out_shape=jax.ShapeDtypeStruct((b, l, w), BF16),
        compiler_params=_params(("parallel",)),
        name="fox_sample",
    )(q, k_cache, v_cache, k_new, v_new, g_cache, g_new)


CONV_PAD = 8


def _mlstm_kernel(u_ref, mv_ref, mo_ref, gat_ref, gt_ref, c0_ref, n0_ref, m0_ref,
                  cw_ref, cb_ref, ng_ref,
                  ob_ref, c_ref, n_ref, m_ref, conv_ref):
    chunk = pl.program_id(1)
    L = u_ref.shape[1]
    hd = M_HEAD_DIM

    @pl.when(chunk == 0)
    def _():
        c_ref[...] = c0_ref[...]
        n_ref[...] = n0_ref[...]
        m_ref[...] = m0_ref[...]
        conv_ref[...] = jnp.zeros_like(conv_ref)

    conv_ref[CONV_PAD:CONV_PAD + L, :] = u_ref[0]
    y = cb_ref[...]
    for j in range(CONV_W):
        lo = CONV_PAD - (CONV_W - 1) + j
        y = y + conv_ref[lo:lo + L, :] * cw_ref[j:j + 1, :]
    tail = conv_ref[L + CONV_PAD - (CONV_W - 1):L + CONV_PAD, :]
    conv_ref[CONV_PAD - (CONV_W - 1):CONV_PAD, :] = tail
    qk = y * _sigmoid(y)

    gat = gat_ref[0]
    gt = gt_ref[0]
    r = lax.broadcasted_iota(jnp.int32, (L, L), 0)
    c = lax.broadcasted_iota(jnp.int32, (L, L), 1)
    causal = c <= r
    b_cols = _dot_exact(causal.astype(F32), gat)
    b_rows = _dot_exact(gt, (r <= c).astype(F32))
    m_prev_all = m_ref[0]
    lane = lax.broadcasted_iota(jnp.int32, (1, LANES), 1)
    m_next_all = jnp.zeros((1, LANES), F32)

    for h in range(M_HEADS):
        hs = slice(h * hd, (h + 1) * hd)
        q32 = qk[:, h * hd:(h + 1) * hd]
        k32 = qk[:, M_WIDTH + h * hd:M_WIDTH + (h + 1) * hd] * (hd ** -0.5)
        q = q32.astype(BF16)
        k = k32.astype(BF16)
        v = mv_ref[0, :, hs]
        bc = b_cols[:, G_LF + h:G_LF + h + 1]
        br = b_rows[M_HEADS + h:M_HEADS + h + 1, :]
        ir = gt[h:h + 1, :]
        ic = gat[:, G_IG + h:G_IG + h + 1]
        m_prev = m_prev_all[:, h:h + 1]
        c_state = c_ref[0, h]
        n_state = n_ref[0, h:h + 1, :]

        d = jnp.where(causal, bc - br + ir, NEG_BIG)
        inter = bc + m_prev
        m_t = jnp.maximum(inter, jnp.max(d, axis=-1, keepdims=True))
        w_inter = jnp.exp(inter - m_t)
        a = jnp.exp(d - m_t) * _dot_nt(q, k)
        num = _dot(a.astype(BF16), v) + w_inter * _dot_nt(q, c_state.astype(BF16))
        den = jnp.sum(a, axis=-1, keepdims=True) + w_inter * jnp.sum(q32 * n_state, axis=-1, keepdims=True)
        hval = num / jnp.maximum(jnp.abs(den), jnp.exp(-m_t))

        m_new = m_t[L - 1:L, :]
        b_last = bc[L - 1:L, :]
        w_state = jnp.exp(b_last + m_prev - m_new)
        w_s = jnp.exp(b_last - bc + ic - m_new)
        vw = (v.astype(F32) * w_s).astype(BF16)
        c_ref[0, h] = w_state * c_state + _dot_tn(vw, k)
        n_ref[0, h:h + 1, :] = w_state * n_state + jnp.sum(k32 * w_s, axis=0, keepdims=True)
        m_next_all = jnp.where(lane == h, m_new, m_next_all)

        mu = jnp.mean(hval, axis=-1, keepdims=True)
        hc = hval - mu
        var = jnp.mean(hc * hc, axis=-1, keepdims=True)
        hn = hc * lax.rsqrt(var + LN_EPS)
        ob_ref[0, :, hs] = (hn * ng_ref[:, hs] * _sigmoid(mo_ref[0, :, hs])).astype(ob_ref.dtype)

    m_ref[0] = m_next_all


def _mlstm(u, mv, mo, gat, gat_t, c0, n0, m0, conv_w, conv_b, norm_g, chunk):
    b, s, _ = u.shape

    def seq(width):
        return pl.BlockSpec((1, chunk, width), lambda bi, ci: (bi, ci, 0))

    def per_batch(shape):
        nd = len(shape)
        return pl.BlockSpec((1,) + tuple(shape[1:]), lambda bi, ci: (bi,) + (0,) * (nd - 1))

    return pl.pallas_call(
        _mlstm_kernel,
        grid=(b, s // chunk),
        in_specs=[seq(2 * M_WIDTH), seq(M_WIDTH), seq(M_WIDTH), seq(LANES),
                  pl.BlockSpec((1, 2 * M_HEADS, chunk), lambda bi, ci: (bi, 0, ci)),
                  per_batch(c0.shape), per_batch(n0.shape), per_batch(m0.shape),
                  _resident_2d(conv_w.shape), _resident_2d(conv_b.shape), _resident_2d(norm_g.shape)],
        out_specs=[seq(M_WIDTH), per_batch(c0.shape), per_batch(n0.shape), per_batch(m0.shape)],
        out_shape=[jax.ShapeDtypeStruct((b, s, M_WIDTH), BF16),
                   jax.ShapeDtypeStruct(c0.shape, F32),
                   jax.ShapeDtypeStruct(n0.shape, F32),
                   jax.ShapeDtypeStruct(m0.shape, F32)],
        scratch_shapes=[pltpu.VMEM((chunk + CONV_PAD, 2 * M_WIDTH), F32)],
        compiler_params=_params(("parallel", "arbitrary")),
        name="mlstm",
    )(u, mv, mo, gat, gat_t, c0, n0, m0, conv_w, conv_b, norm_g)


def _resident_2d(shape):
    return pl.BlockSpec(shape, lambda bi, ci: (0, 0))


def _merge_kernel(x_ref, oa_ref, ob_ref, gab_ref, g_ref, wa_ref, wb_ref, wo_ref, lng_ref, lnb_ref, o_ref):
    x = x_ref[...]
    ga = gab_ref[:, :D_MODEL]
    gb = gab_ref[:, D_MODEL:]
    m = _sigmoid(ga) * _dot(oa_ref[...], wa_ref[...]) + _sigmoid(gb) * _dot(ob_ref[...], wb_ref[...])
    mix = _dot(m.astype(BF16), wo_ref[...])
    y = ALPHA * x + g_ref[0] * mix
    o_ref[...] = _layer_norm(y, lng_ref[...], lnb_ref[...])


def _merge(x, oa, ob, gab, mod, tiles_per_batch, tm, w_a, w_b, w_out, ln_g, ln_b):
    rows = x.shape[0]

    def rs(width):
        return pl.BlockSpec((tm, width), lambda i: (i, 0))

    return pl.pallas_call(
        _merge_kernel,
        grid=(rows // tm,),
        in_specs=[rs(D_MODEL), rs(A_WIDTH), rs(M_WIDTH), rs(2 * D_MODEL),
                  _mod_spec(mod, 5, tiles_per_batch),
                  _resident(w_a.shape), _resident(w_b.shape), _resident(w_out.shape),
                  _resident(ln_g.shape), _resident(ln_b.shape)],
        out_specs=rs(D_MODEL),
        out_shape=jax.ShapeDtypeStruct((rows, D_MODEL), F32),
        compiler_params=_params(("parallel",)),
        name="merge",
    )(x, oa, ob, gab, mod, w_a, w_b, w_out, ln_g, ln_b)


PROMPT_TM = 512
ATTN_BLOCK = 512
PROMPT_CHUNK = 256
SAMPLE_CHUNK = 128


def _repack_in_proj(w_in, b_in):
    sizes = (A_WIDTH, A_WIDTH, A_WIDTH, A_HEADS, M_WIDTH, M_WIDTH, M_WIDTH, M_HEADS, M_HEADS, M_WIDTH,
             D_MODEL, D_MODEL)
    off = [int(o) for o in np.cumsum((0,) + sizes)]

    def cols(a, i):
        return a[..., off[i]:off[i + 1]]

    main = (0, 1, 2, 4, 5, 6, 9, 10, 11)
    small = (3, 7, 8)
    w_main = jnp.concatenate([cols(w_in, i) for i in main], axis=-1).astype(BF16)
    b_main = jnp.concatenate([cols(b_in, i) for i in main], axis=-1)[None, :]
    n_small = A_HEADS + 2 * M_HEADS
    w_small = jnp.pad(jnp.concatenate([cols(w_in, i) for i in small], axis=-1),
                      ((0, 0), (0, LANES - n_small))).astype(BF16)
    b_small = jnp.pad(jnp.concatenate([cols(b_in, i) for i in small], axis=-1), (0, LANES - n_small))[None, :]
    return w_main, b_main, w_small, b_small


def kernel(x_prompt, x_sample, cache_fox_k, cache_fox_v, cache_fox_logf, state_mlstm_C, state_mlstm_n, state_mlstm_m, state_conv, c_prompt, c_sample, w_ada, b_ada, ffn1_w_gu, ffn1_w_down, w_in, b_in, conv_w, conv_b, mlstm_norm_g, w_branch_a, w_branch_b, w_out, ffn2_w_gu, ffn2_w_down, ln_g, ln_b):
    assert w_ada.shape[0] == DEPTH == 1
    B, S, D = x_prompt.shape
    DB, DS, _ = x_sample.shape
    P = cache_fox_k.shape[2]
    l = 0

    w_main, b_main, w_small, b_small = _repack_in_proj(w_in[l], b_in[l])
    wgu1, wd1 = ffn1_w_gu[l].astype(BF16), ffn1_w_down[l].astype(BF16)
    wgu2, wd2 = ffn2_w_gu[l].astype(BF16), ffn2_w_down[l].astype(BF16)
    w_a, w_b, w_o = w_branch_a[l].astype(BF16), w_branch_b[l].astype(BF16), w_out[l].astype(BF16)
    lng = [ln_g[l, i][None, :] for i in range(3)]
    lnb = [ln_b[l, i][None, :] for i in range(3)]
    cw, cb, ng = conv_w[l], conv_b[l][None, :], mlstm_norm_g[l][None, :]

    n_c = B + DB
    c_rows = -(-n_c // 8) * 8
    c_all = jnp.pad(jnp.concatenate([c_prompt, c_sample], axis=0), ((0, c_rows - n_c), (0, 0)))
    mod = _ada(c_all, w_ada[l], b_ada[l][None, :])
    mod_p = mod[:B][:, None, :]
    mod_s = jnp.repeat(mod[B:n_c], DS, axis=0)[None]

    rows_s = DB * DS
    tiles_pb = S // PROMPT_TM

    xp = x_prompt.reshape(B * S, D)
    xp = _ffn(xp, mod_p, 0, tiles_pb, PROMPT_TM, wgu1, wd1, lng[0], lnb[0])
    q, kf, vf, kb, vb, gat, u, mv, mo, gab = _inproj(xp, mod_p, tiles_pb, PROMPT_TM,
                                                    w_main, b_main, w_small, b_small)
    gat3 = gat.reshape(B, S, LANES)
    logf_t = jnp.swapaxes(gat3[:, :, :A_HEADS], 1, 2).reshape(B * A_HEADS, S)
    f_rows = _cumsum_lanes(logf_t, ATTN_BLOCK).reshape(B, A_HEADS // 2, 2, S)
    o_a = _fox_prompt(q.reshape(B, S, A_WIDTH), kb.reshape(B, S, A_WIDTH), vb.reshape(B, S, A_WIDTH),
                      f_rows, ATTN_BLOCK)
    gat_t = jnp.swapaxes(gat3[:, :, G_IG:G_IG + 2 * M_HEADS], 1, 2)
    o_b, C_p, n_p, m_p = _mlstm(
        u.reshape(B, S, 2 * M_WIDTH), mv.reshape(B, S, M_WIDTH), mo.reshape(B, S, M_WIDTH), gat3, gat_t,
        jnp.zeros((B, M_HEADS, M_HEAD_DIM, M_HEAD_DIM), F32), jnp.zeros((B, M_HEADS, M_HEAD_DIM), F32),
        jnp.zeros((B, 1, LANES), F32), cw, cb, ng, PROMPT_CHUNK)
    xp = _merge(xp, o_a.reshape(B * S, A_WIDTH), o_b.reshape(B * S, M_WIDTH), gab, mod_p, tiles_pb,
                PROMPT_TM, w_a, w_b, w_o, lng[1], lnb[1])
    xp = _ffn(xp, mod_p, 6, tiles_pb, PROMPT_TM, wgu2, wd2, lng[2], lnb[2])

    prompt_state = (kf.reshape(1, B, S, A_HEADS, A_HEAD_DIM), vf.reshape(1, B, S, A_HEADS, A_HEAD_DIM),
                    gat3[None, :, :, :A_HEADS], C_p[None], n_p[None], m_p[None, :, 0, :M_HEADS],
                    u.reshape(B, S, 2 * M_WIDTH)[None, :, S - (CONV_W - 1):, :])

    xs = x_sample.reshape(rows_s, D)
    xs = _ffn(xs, mod_s, 0, 1, rows_s, wgu1, wd1, lng[0], lnb[0])
    q, kf, vf, kb, vb, gat, u, mv, mo, gab = _inproj(xs, mod_s, 1, rows_s, w_main, b_main, w_small, b_small)
    gat3 = gat.reshape(DB, DS, LANES)

    logf_all = jnp.concatenate([cache_fox_logf[l], gat3[:, :, :A_HEADS]], axis=1)
    total = P + LANES
    logf_all = jnp.pad(jnp.swapaxes(logf_all, 1, 2), ((0, 0), (0, 0), (0, total - P - DS)))
    f_all = _cumsum_lanes(logf_all.reshape(DB * A_HEADS, total), LANES).reshape(DB, A_HEADS, total)
    g_all = f_all - f_all[:, :, P - 1:P]
    pad_new = ((0, 0), (0, LANES - DS), (0, 0))
    o_a = _fox_sample(q.reshape(DB, DS, A_WIDTH),
                      cache_fox_k[l].reshape(DB, P, A_WIDTH), cache_fox_v[l].reshape(DB, P, A_WIDTH),
                      jnp.pad(kb.reshape(DB, DS, A_WIDTH), pad_new), jnp.pad(vb.reshape(DB, DS, A_WIDTH), pad_new),
                      g_all[:, :, :P], g_all[:, :, P:], DS)

    lead = SAMPLE_CHUNK - DS
    front = ((0, 0), (lead, 0), (0, 0))
    u3 = jnp.concatenate([jnp.zeros((DB, lead - (CONV_W - 1), 2 * M_WIDTH), F32), state_conv[l],
                          u.reshape(DB, DS, 2 * M_WIDTH)], axis=1)
    lane = jnp.arange(LANES)
    noop = jnp.where((lane >= G_IG) & (lane < G_LF), NEG_BIG, 0.0).astype(F32)
    gat_pad = jnp.concatenate([jnp.broadcast_to(noop, (DB, lead, LANES)), gat3], axis=1)
    gat_t = jnp.swapaxes(gat_pad[:, :, G_IG:G_IG + 2 * M_HEADS], 1, 2)
    m0 = jnp.pad(state_mlstm_m[l], ((0, 0), (0, LANES - M_HEADS)))[:, None, :]
    o_b, C_s, n_s, m_s = _mlstm(
        u3, jnp.pad(mv.reshape(DB, DS, M_WIDTH), front), jnp.pad(mo.reshape(DB, DS, M_WIDTH), front),
        gat_pad, gat_t, state_mlstm_C[l], state_mlstm_n[l], m0, cw, cb, ng, SAMPLE_CHUNK)
    xs = _merge(xs, o_a.reshape(rows_s, A_WIDTH), o_b[:, lead:, :].reshape(rows_s, M_WIDTH), gab, mod_s, 1,
                rows_s, w_a, w_b, w_o, lng[1], lnb[1])
    xs = _ffn(xs, mod_s, 6, 1, rows_s, wgu2, wd2, lng[2], lnb[2])

    sample_state = (kf.reshape(1, DB, DS, A_HEADS, A_HEAD_DIM), vf.reshape(1, DB, DS, A_HEADS, A_HEAD_DIM),
                    gat3[None, :, :, :A_HEADS], C_s[None], n_s[None], m_s[None, :, 0, :M_HEADS],
                    u3[None, :, SAMPLE_CHUNK - (CONV_W - 1):, :])

    return (xp.reshape(B, S, D), xs.reshape(DB, DS, D)) + prompt_state + sample_state
```

```python
import functools

import jax
import jax.numpy as jnp
import numpy as np
from jax import lax
from jax.experimental import pallas as pl
from jax.experimental.pallas import tpu as pltpu

F32 = jnp.float32
BF16 = jnp.bfloat16

D_MODEL = 1024
A_HEADS = 8
A_HEAD_DIM = 64
A_WIDTH = A_HEADS * A_HEAD_DIM
M_HEADS = 4
M_HEAD_DIM = 128
M_WIDTH = M_HEADS * M_HEAD_DIM
CONV_W = 4
D_FF = 2816
LN_EPS = 1e-5
DEPTH = 1
ALPHA = (2.0 * DEPTH) ** 0.25
LANES = 128
LOG2E = 1.4426950408889634
NEG_BIG = -1e30
VMEM_LIMIT = 56 * 1024 * 1024

MAIN_COLS = 3 * A_WIDTH + 4 * M_WIDTH + 2 * D_MODEL
G_IG = A_HEADS
G_LF = A_HEADS + M_HEADS


def _params(sem):
    return pltpu.CompilerParams(dimension_semantics=sem, vmem_limit_bytes=VMEM_LIMIT)


def _resident(shape):
    nd = len(shape)
    return pl.BlockSpec(shape, lambda *_: (0,) * nd, pipeline_mode=pl.Buffered(1))


def _sigmoid(x):
    return 1.0 / (1.0 + jnp.exp(-x))


def _layer_norm(y, g, b):
    mu = jnp.mean(y, axis=-1, keepdims=True)
    yc = y - mu
    var = jnp.mean(yc * yc, axis=-1, keepdims=True)
    return yc * lax.rsqrt(var + LN_EPS) * g + b


def _dot(a, b):
    return jnp.dot(a, b, preferred_element_type=F32)


def _dot_nt(a, b):
    return lax.dot_general(a, b, (((1,), (1,)), ((), ())), preferred_element_type=F32)


def _dot_tn(a, b):
    return lax.dot_general(a, b, (((0,), (0,)), ((), ())), preferred_element_type=F32)


def _dot_exact(a, b):
    return jnp.dot(a, b, preferred_element_type=F32, precision=lax.Precision.HIGHEST)


def _ada_kernel(c_ref, w_ref, b_ref, o_ref):
    c = c_ref[...]
    o_ref[...] = _dot(c * _sigmoid(c), w_ref[...]) + b_ref[...]


def _ada(c_all, w_ada, b_ada):
    rows = c_all.shape[0]
    n = w_ada.shape[1]
    tn = 1024
    return pl.pallas_call(
        _ada_kernel,
        grid=(n // tn,),
        in_specs=[pl.BlockSpec((rows, D_MODEL), lambda j: (0, 0)),
                  pl.BlockSpec((D_MODEL, tn), lambda j: (0, j)),
                  pl.BlockSpec((1, tn), lambda j: (0, j))],
        out_specs=pl.BlockSpec((rows, tn), lambda j: (0, j)),
        out_shape=jax.ShapeDtypeStruct((rows, n), F32),
        compiler_params=_params(("parallel",)),
        name="ada",
    )(c_all, w_ada, b_ada)


FFN_CHUNK = 256


def _ffn_kernel(x_ref, sh_ref, sc_ref, g_ref, wgu_ref, wd_ref, lng_ref, lnb_ref, o_ref):
    x = x_ref[...]
    h = (x * (1.0 + sc_ref[0]) + sh_ref[0]).astype(BF16)
    acc = jnp.zeros(x.shape, F32)
    for c in range(D_FF // FFN_CHUNK):
        lo = c * FFN_CHUNK
        gate = _dot(h, wgu_ref[:, lo:lo + FFN_CHUNK])
        up = _dot(h, wgu_ref[:, D_FF + lo:D_FF + lo + FFN_CHUNK])
        act = (gate * _sigmoid(gate) * up).astype(BF16)
        acc = acc + _dot(act, wd_ref[lo:lo + FFN_CHUNK, :])
    y = ALPHA * x + 0.5 * g_ref[0] * acc
    o_ref[...] = _layer_norm(y, lng_ref[...], lnb_ref[...])


def _mod_spec(mod, group, tiles_per_batch):
    return pl.BlockSpec((1, mod.shape[1], D_MODEL), lambda i: (i // tiles_per_batch, 0, group))


def _ffn(x, mod, group0, tiles_per_batch, tm, w_gu, w_down, ln_g, ln_b):
    rows = x.shape[0]
    row_spec = pl.BlockSpec((tm, D_MODEL), lambda i: (i, 0))
    return pl.pallas_call(
        _ffn_kernel,
        grid=(rows // tm,),
        in_specs=[row_spec,
                  _mod_spec(mod, group0, tiles_per_batch),
                  _mod_spec(mod, group0 + 1, tiles_per_batch),
                  _mod_spec(mod, group0 + 2, tiles_per_batch),
                  _resident(w_gu.shape), _resident(w_down.shape),
                  _resident(ln_g.shape), _resident(ln_b.shape)],
        out_specs=row_spec,
        out_shape=jax.ShapeDtypeStruct((rows, D_MODEL), F32),
        compiler_params=_params(("parallel",)),
        name="ffn",
    )(x, mod, mod, mod, w_gu, w_down, ln_g, ln_b)


def _inproj_kernel(x_ref, sh_ref, sc_ref, wm_ref, bm_ref, ws_ref, bs_ref,
                   q_ref, kf_ref, vf_ref, kb_ref, vb_ref, gat_ref, u_ref, mv_ref, mo_ref, gab_ref):
    x = x_ref[...]
    h = (x * (1.0 + sc_ref[0]) + sh_ref[0]).astype(BF16)

    def proj(lo, hi):
        return _dot(h, wm_ref[:, lo:hi]) + bm_ref[:, lo:hi]

    aw, mw = A_WIDTH, M_WIDTH
    q_ref[...] = (proj(0, aw) * (LOG2E * A_HEAD_DIM ** -0.5)).astype(BF16)
    k = proj(aw, 2 * aw)
    kf_ref[...] = k
    kb_ref[...] = k.astype(BF16)
    v = proj(2 * aw, 3 * aw)
    vf_ref[...] = v
    vb_ref[...] = v.astype(BF16)
    o = 3 * aw
    u_ref[...] = proj(o, o + 2 * mw)
    mv_ref[...] = proj(o + 2 * mw, o + 3 * mw).astype(BF16)
    mo_ref[...] = proj(o + 3 * mw, o + 4 * mw)
    gab_ref[...] = proj(o + 4 * mw, o + 4 * mw + 2 * D_MODEL)

    z = _dot(h, ws_ref[...]) + bs_ref[...]
    log_sig = jnp.minimum(z, 0.0) - jnp.log1p(jnp.exp(-jnp.abs(z)))
    lane = lax.broadcasted_iota(jnp.int32, z.shape, 1)
    is_input_gate = (lane >= G_IG) & (lane < G_LF)
    gat_ref[...] = jnp.where(is_input_gate, z, log_sig)


def _inproj(x, mod, tiles_per_batch, tm, w_main, b_main, w_small, b_small):
    rows = x.shape[0]

    def rs(width):
        return pl.BlockSpec((tm, width), lambda i: (i, 0))

    def out(width, dt):
        return jax.ShapeDtypeStruct((rows, width), dt)

    return pl.pallas_call(
        _inproj_kernel,
        grid=(rows // tm,),
        in_specs=[rs(D_MODEL),
                  _mod_spec(mod, 3, tiles_per_batch),
                  _mod_spec(mod, 4, tiles_per_batch),
                  _resident(w_main.shape), _resident(b_main.shape),
                  _resident(w_small.shape), _resident(b_small.shape)],
        out_specs=[rs(A_WIDTH), rs(A_WIDTH), rs(A_WIDTH), rs(A_WIDTH), rs(A_WIDTH),
                   rs(LANES), rs(2 * M_WIDTH), rs(M_WIDTH), rs(M_WIDTH), rs(2 * D_MODEL)],
        out_shape=[out(A_WIDTH, BF16), out(A_WIDTH, F32), out(A_WIDTH, F32),
                   out(A_WIDTH, BF16), out(A_WIDTH, BF16), out(LANES, F32),
                   out(2 * M_WIDTH, F32), out(M_WIDTH, BF16), out(M_WIDTH, F32),
                   out(2 * D_MODEL, F32)],
        compiler_params=_params(("parallel",)),
        name="inproj",
    )(x, mod, mod, w_main, b_main, w_small, b_small)


def _cumsum_kernel(x_ref, o_ref, carry_ref, *, scale):
    @pl.when(pl.program_id(0) == 0)
    def _():
        carry_ref[...] = jnp.zeros_like(carry_ref)

    t = x_ref.shape[1]
    upper = (lax.broadcasted_iota(jnp.int32, (t, t), 0)
             <= lax.broadcasted_iota(jnp.int32, (t, t), 1)).astype(F32)
    out = _dot_exact(x_ref[...], upper) + carry_ref[...]
    o_ref[...] = out * scale
    carry_ref[...] = out[:, t - 1:t]


def _cumsum_lanes(x, block, scale):
    rows, total = x.shape
    spec = pl.BlockSpec((rows, block), lambda j: (0, j))
    return pl.pallas_call(
        functools.partial(_cumsum_kernel, scale=scale),
        grid=(total // block,),
        in_specs=[spec],
        out_specs=spec,
        out_shape=jax.ShapeDtypeStruct((rows, total), F32),
        scratch_shapes=[pltpu.VMEM((rows, 1), F32)],
        compiler_params=_params(("arbitrary",)),
        name="cumsum",
    )(x)


HALF = LANES // 2
ROW_CHUNK = 128


def _head_masks(q):
    lane = lax.broadcasted_iota(jnp.int32, q.shape, 1)
    zero = jnp.zeros_like(q)
    return jnp.where(lane < HALF, q, zero), jnp.where(lane >= HALF, q, zero)


def _fox_prompt_kernel(q_ref, k_ref, v_ref, f_ref, o_ref, sa_ref, sb_ref, m_ref, l_ref, acc_ref, *, tq, tk):
    qi = pl.program_id(2)
    per_q = tq // tk
    q_heads = _head_masks(q_ref[0])
    m_ref[...] = jnp.full(m_ref.shape, NEG_BIG, F32)
    l_ref[...] = jnp.zeros(l_ref.shape, F32)
    acc_ref[...] = jnp.zeros(acc_ref.shape, F32)

    def scores(j, s_ref):
        start = pl.multiple_of(j * tk, tk)
        k = k_ref[0, pl.ds(start, tk), :]
        for hh in range(2):
            s_ref[hh] = _dot_nt(q_heads[hh], k) - f_ref[0, 0, hh:hh + 1, pl.ds(start, tk)]

    def process(j, s_ref, diag_offset=None):
        start = pl.multiple_of(j * tk, tk)
        v = v_ref[0, pl.ds(start, tk), :]
        for r0 in range(0, tq, ROW_CHUNK):
            if diag_offset is not None and r0 + ROW_CHUNK <= diag_offset:
                continue
            rows = slice(r0, r0 + ROW_CHUNK)
            for hh in range(2):
                s = s_ref[hh, rows, :]
                if diag_offset is not None and r0 < diag_offset + tk:
                    row = lax.broadcasted_iota(jnp.int32, s.shape, 0) + r0
                    col = lax.broadcasted_iota(jnp.int32, s.shape, 1) + diag_offset
                    s = jnp.where(col <= row, s, NEG_BIG)
                m_old = m_ref[hh, rows, :]
                m_new = jnp.maximum(m_old, jnp.max(s, axis=-1, keepdims=True))
                alpha = jnp.exp2(m_old - m_new)
                p = jnp.exp2(s - pltpu.repeat(m_new, tk // LANES, axis=1))
                p_lanes = p[:, :LANES]
                for t in range(1, tk // LANES):
                    p_lanes = p_lanes + p[:, t * LANES:(t + 1) * LANES]
                l_ref[hh, rows, :] = alpha * l_ref[hh, rows, :] + p_lanes
                acc_ref[hh, rows, :] = alpha * acc_ref[hh, rows, :] + _dot(p.astype(BF16), v)
                m_ref[hh, rows, :] = m_new

    def body(t, carry):
        j = t * 2
        scores(j + 1, sb_ref)
        process(j, sa_ref)
        scores(j + 2, sa_ref)
        process(j + 1, sb_ref)
        return carry

    first_diag = qi * per_q
    scores(0, sa_ref)
    lax.fori_loop(0, first_diag // 2, body, 0)
    process(first_diag, sa_ref, diag_offset=0)
    for d in range(1, per_q):
        scores(first_diag + d, sb_ref)
        process(first_diag + d, sb_ref, diag_offset=d * tk)

    lane = lax.broadcasted_iota(jnp.int32, (tq, LANES), 1)
    outs = [acc_ref[hh] / jnp.sum(l_ref[hh], axis=-1, keepdims=True) for hh in range(2)]
    o_ref[0] = jnp.where(lane < HALF, outs[0], outs[1]).astype(o_ref.dtype)


def _fox_prompt(q, k, v, f_rows, tq, tk):
    b, s, _ = q.shape
    assert tq % (2 * tk) == 0 and tk % LANES == 0 and s % tq == 0
    pairs = A_HEADS // 2
    q_spec = pl.BlockSpec((1, tq, LANES), lambda bi, p, i: (bi, i, p))
    kv_spec = pl.BlockSpec((1, s, LANES), lambda bi, p, i: (bi, 0, p))
    stat = pltpu.VMEM((2, tq, LANES), F32)
    return pl.pallas_call(
        functools.partial(_fox_prompt_kernel, tq=tq, tk=tk),
        grid=(b, pairs, s // tq),
        in_specs=[q_spec, kv_spec, kv_spec,
                  pl.BlockSpec((1, 1, 2, s), lambda bi, p, i: (bi, p, 0, 0))],
        out_specs=q_spec,
        out_shape=jax.ShapeDtypeStruct((b, s, A_WIDTH), BF16),
        scratch_shapes=[pltpu.VMEM((2, tq, tk), F32), pltpu.VMEM((2, tq, tk), F32), stat, stat, stat],
        compiler_params=_params(("parallel", "parallel", "arbitrary")),
        name="fox_prompt",
    )(q, k, v, f_rows)


def _fox_sample_kernel(q_ref, kc_ref, vc_ref, kn_ref, vn_ref, gc_ref, gn_ref, o_ref, *, n_new):
    rows = q_ref.shape[1]
    pad = kn_ref.shape[1]
    row = lax.broadcasted_iota(jnp.int32, (rows, pad), 0)
    col = lax.broadcasted_iota(jnp.int32, (rows, pad), 1)
    visible = (col <= row) & (col < n_new)
    lane = lax.broadcasted_iota(jnp.int32, (rows, LANES), 1)
    for p in range(A_HEADS // 2):
        cols = slice(p * LANES, (p + 1) * LANES)
        q_heads = _head_masks(q_ref[0, :, cols])
        kc = kc_ref[0, :, cols].astype(BF16)
        vc = vc_ref[0, :, cols].astype(BF16)
        kn = kn_ref[0, :, cols]
        vn = vn_ref[0, :, cols]
        outs = []
        for hh in range(2):
            h = 2 * p + hh
            s_c = _dot_nt(q_heads[hh], kc) - gc_ref[0, h:h + 1, :]
            s_n = jnp.where(visible, _dot_nt(q_heads[hh], kn) - gn_ref[0, h:h + 1, :], NEG_BIG)
            m = jnp.maximum(jnp.max(s_c, axis=-1, keepdims=True), jnp.max(s_n, axis=-1, keepdims=True))
            p_c = jnp.exp2(s_c - m)
            p_n = jnp.exp2(s_n - m)
            l = jnp.sum(p_c, axis=-1, keepdims=True) + jnp.sum(p_n, axis=-1, keepdims=True)
            outs.append((_dot(p_c.astype(BF16), vc) + _dot(p_n.astype(BF16), vn)) / l)
        o_ref[0, :, cols] = jnp.where(lane < HALF, outs[0], outs[1]).astype(o_ref.dtype)


def _fox_sample(q, k_cache, v_cache, k_new, v_new, g_cache, g_new, n_new):
    b, l, w = q.shape
    p = k_cache.shape[1]
    pad = k_new.shape[1]

    def spec(r, c):
        return pl.BlockSpec((1, r, c), lambda i: (i, 0, 0))

    return pl.pallas_call(
        functools.partial(_fox_sample_kernel, n_new=n_new),
        grid=(b,),
        in_specs=[spec(l, w), spec(p, w), spec(p, w), spec(pad, w), spec(pad, w),
                  spec(A_HEADS, p), spec(A_HEADS, pad)],
        out_specs=spec(l, w),
        out_shape=jax.ShapeDtypeStruct((b, l, w), BF16),
        compiler_params=_params(("parallel",)),
        name="fox_sample",
    )(q, k_cache, v_cache, k_new, v_new, g_cache, g_new)


CONV_PAD = 8


def _mlstm_kernel(u_ref, mv_ref, mo_ref, gat_ref, gt_ref, c0_ref, n0_ref, m0_ref,
                  cw_ref, cb_ref, ng_ref,
                  ob_ref, c_ref, n_ref, m_ref, conv_ref):
    chunk = pl.program_id(1)
    L = u_ref.shape[1]
    hd = M_HEAD_DIM

    @pl.when(chunk == 0)
    def _():
        c_ref[...] = c0_ref[...]
        n_ref[...] = n0_ref[...]
        m_ref[...] = m0_ref[...]
        conv_ref[...] = jnp.zeros_like(conv_ref)

    conv_ref[CONV_PAD:CONV_PAD + L, :] = u_ref[0]
    y = cb_ref[...]
    for j in range(CONV_W):
        lo = CONV_PAD - (CONV_W - 1) + j
        y = y + conv_ref[lo:lo + L, :] * cw_ref[j:j + 1, :]
    tail = conv_ref[L + CONV_PAD - (CONV_W - 1):L + CONV_PAD, :]
    conv_ref[CONV_PAD - (CONV_W - 1):CONV_PAD, :] = tail
    qk = y * _sigmoid(y)

    gat = gat_ref[0]
    gt = gt_ref[0]
    r = lax.broadcasted_iota(jnp.int32, (L, L), 0)
    c = lax.broadcasted_iota(jnp.int32, (L, L), 1)
    causal = c <= r
    b_cols = _dot_exact(causal.astype(F32), gat)
    b_rows = _dot_exact(gt, (r <= c).astype(F32))
    m_prev_all = m_ref[0]
    lane = lax.broadcasted_iota(jnp.int32, (1, LANES), 1)
    m_next_all = jnp.zeros((1, LANES), F32)

    for h in range(M_HEADS):
        hs = slice(h * hd, (h + 1) * hd)
        q32 = qk[:, h * hd:(h + 1) * hd]
        k32 = qk[:, M_WIDTH + h * hd:M_WIDTH + (h + 1) * hd] * (hd ** -0.5)
        q = q32.astype(BF16)
        k = k32.astype(BF16)
        v = mv_ref[0, :, hs]
        bc = b_cols[:, G_LF + h:G_LF + h + 1]
        br = b_rows[M_HEADS + h:M_HEADS + h + 1, :]
        ir = gt[h:h + 1, :]
        ic = gat[:, G_IG + h:G_IG + h + 1]
        m_prev = m_prev_all[:, h:h + 1]
        c_state = c_ref[0, h]
        n_state = n_ref[0, h:h + 1, :]

        d = jnp.where(causal, bc - br + ir, NEG_BIG)
        inter = bc + m_prev
        m_t = jnp.maximum(inter, jnp.max(d, axis=-1, keepdims=True))
        w_inter = jnp.exp(inter - m_t)
        a = jnp.exp(d - m_t) * _dot_nt(q, k)
        num = _dot(a.astype(BF16), v) + w_inter * _dot_nt(q, c_state.astype(BF16))
        den = jnp.sum(a, axis=-1, keepdims=True) + w_inter * jnp.sum(q32 * n_state, axis=-1, keepdims=True)
        hval = num / jnp.maximum(jnp.abs(den), jnp.exp(-m_t))

        m_new = m_t[L - 1:L, :]
        b_last = bc[L - 1:L, :]
        w_state = jnp.exp(b_last + m_prev - m_new)
        w_s = jnp.exp(b_last - bc + ic - m_new)
        vw = (v.astype(F32) * w_s).astype(BF16)
        c_ref[0, h] = w_state * c_state + _dot_tn(vw, k)
        n_ref[0, h:h + 1, :] = w_state * n_state + jnp.sum(k32 * w_s, axis=0, keepdims=True)
        m_next_all = jnp.where(lane == h, m_new, m_next_all)

        mu = jnp.mean(hval, axis=-1, keepdims=True)
        hc = hval - mu
        var = jnp.mean(hc * hc, axis=-1, keepdims=True)
        hn = hc * lax.rsqrt(var + LN_EPS)
        ob_ref[0, :, hs] = (hn * ng_ref[:, hs] * _sigmoid(mo_ref[0, :, hs])).astype(ob_ref.dtype)

    m_ref[0] = m_next_all


def _mlstm(u, mv, mo, gat, gat_t, c0, n0, m0, conv_w, conv_b, norm_g, chunk):
    b, s, _ = u.shape

    def seq(width):
        return pl.BlockSpec((1, chunk, width), lambda bi, ci: (bi, ci, 0))

    def per_batch(shape):
        nd = len(shape)
        return pl.BlockSpec((1,) + tuple(shape[1:]), lambda bi, ci: (bi,) + (0,) * (nd - 1))

    return pl.pallas_call(
        _mlstm_kernel,
        grid=(b, s // chunk),
        in_specs=[seq(2 * M_WIDTH), seq(M_WIDTH), seq(M_WIDTH), seq(LANES),
                  pl.BlockSpec((1, 2 * M_HEADS, chunk), lambda bi, ci: (bi, 0, ci)),
                  per_batch(c0.shape), per_batch(n0.shape), per_batch(m0.shape),
                  _resident_2d(conv_w.shape), _resident_2d(conv_b.shape), _resident_2d(norm_g.shape)],
        out_specs=[seq(M_WIDTH), per_batch(c0.shape), per_batch(n0.shape), per_batch(m0.shape)],
        out_shape=[jax.ShapeDtypeStruct((b, s, M_WIDTH), BF16),
                   jax.ShapeDtypeStruct(c0.shape, F32),
                   jax.ShapeDtypeStruct(n0.shape, F32),
                   jax.ShapeDtypeStruct(m0.shape, F32)],
        scratch_shapes=[pltpu.VMEM((chunk + CONV_PAD, 2 * M_WIDTH), F32)],
        compiler_params=_params(("parallel", "arbitrary")),
        name="mlstm",
    )(u, mv, mo, gat, gat_t, c0, n0, m0, conv_w, conv_b, norm_g)


def _resident_2d(shape):
    return pl.BlockSpec(shape, lambda bi, ci: (0, 0))


def _merge_kernel(x_ref, oa_ref, ob_ref, gab_ref, g_ref, wa_ref, wb_ref, wo_ref, lng_ref, lnb_ref, o_ref):
    x = x_ref[...]
    ga = gab_ref[:, :D_MODEL]
    gb = gab_ref[:, D_MODEL:]
    m = _sigmoid(ga) * _dot(oa_ref[...], wa_ref[...]) + _sigmoid(gb) * _dot(ob_ref[...], wb_ref[...])
    mix = _dot(m.astype(BF16), wo_ref[...])
    y = ALPHA * x + g_ref[0] * mix
    o_ref[...] = _layer_norm(y, lng_ref[...], lnb_ref[...])


def _merge(x, oa, ob, gab, mod, tiles_per_batch, tm, w_a, w_b, w_out, ln_g, ln_b):
    rows = x.shape[0]

    def rs(width):
        return pl.BlockSpec((tm, width), lambda i: (i, 0))

    return pl.pallas_call(
        _merge_kernel,
        grid=(rows // tm,),
        in_specs=[rs(D_MODEL), rs(A_WIDTH), rs(M_WIDTH), rs(2 * D_MODEL),
                  _mod_spec(mod, 5, tiles_per_batch),
                  _resident(w_a.shape), _resident(w_b.shape), _resident(w_out.shape),
                  _resident(ln_g.shape), _resident(ln_b.shape)],
        out_specs=rs(D_MODEL),
        out_shape=jax.ShapeDtypeStruct((rows, D_MODEL), F32),
        compiler_params=_params(("parallel",)),
        name="merge",
    )(x, oa, ob, gab, mod, w_a, w_b, w_out, ln_g, ln_b)


PROMPT_TM = 512
ATTN_TQ = 512
ATTN_TK = 256
PROMPT_CHUNK = 256
SAMPLE_CHUNK = 128


def _repack_in_proj(w_in, b_in):
    sizes = (A_WIDTH, A_WIDTH, A_WIDTH, A_HEADS, M_WIDTH, M_WIDTH, M_WIDTH, M_HEADS, M_HEADS, M_WIDTH,
             D_MODEL, D_MODEL)
    off = [int(o) for o in np.cumsum((0,) + sizes)]

    def cols(a, i):
        return a[..., off[i]:off[i + 1]]

    main = (0, 1, 2, 4, 5, 6, 9, 10, 11)
    small = (3, 7, 8)
    w_main = jnp.concatenate([cols(w_in, i) for i in main], axis=-1).astype(BF16)
    b_main = jnp.concatenate([cols(b_in, i) for i in main], axis=-1)[None, :]
    n_small = A_HEADS + 2 * M_HEADS
    w_small = jnp.pad(jnp.concatenate([cols(w_in, i) for i in small], axis=-1),
                      ((0, 0), (0, LANES - n_small))).astype(BF16)
    b_small = jnp.pad(jnp.concatenate([cols(b_in, i) for i in small], axis=-1), (0, LANES - n_small))[None, :]
    return w_main, b_main, w_small, b_small


def kernel(x_prompt, x_sample, cache_fox_k, cache_fox_v, cache_fox_logf, state_mlstm_C, state_mlstm_n, state_mlstm_m, state_conv, c_prompt, c_sample, w_ada, b_ada, ffn1_w_gu, ffn1_w_down, w_in, b_in, conv_w, conv_b, mlstm_norm_g, w_branch_a, w_branch_b, w_out, ffn2_w_gu, ffn2_w_down, ln_g, ln_b):
    assert w_ada.shape[0] == DEPTH == 1
    B, S, D = x_prompt.shape
    DB, DS, _ = x_sample.shape
    P = cache_fox_k.shape[2]
    l = 0

    w_main, b_main, w_small, b_small = _repack_in_proj(w_in[l], b_in[l])
    wgu1, wd1 = ffn1_w_gu[l].astype(BF16), ffn1_w_down[l].astype(BF16)
    wgu2, wd2 = ffn2_w_gu[l].astype(BF16), ffn2_w_down[l].astype(BF16)
    w_a, w_b, w_o = w_branch_a[l].astype(BF16), w_branch_b[l].astype(BF16), w_out[l].astype(BF16)
    lng = [ln_g[l, i][None, :] for i in range(3)]
    lnb = [ln_b[l, i][None, :] for i in range(3)]
    cw, cb, ng = conv_w[l], conv_b[l][None, :], mlstm_norm_g[l][None, :]

    n_c = B + DB
    c_rows = -(-n_c // 8) * 8
    c_all = jnp.pad(jnp.concatenate([c_prompt, c_sample], axis=0), ((0, c_rows - n_c), (0, 0)))
    mod = _ada(c_all, w_ada[l], b_ada[l][None, :])
    mod_p = mod[:B][:, None, :]
    mod_s = jnp.repeat(mod[B:n_c], DS, axis=0)[None]

    rows_s = DB * DS
    tiles_pb = S // PROMPT_TM

    xp = x_prompt.reshape(B * S, D)
    xp = _ffn(xp, mod_p, 0, tiles_pb, PROMPT_TM, wgu1, wd1, lng[0], lnb[0])
    q, kf, vf, kb, vb, gat, u, mv, mo, gab = _inproj(xp, mod_p, tiles_pb, PROMPT_TM,
                                                    w_main, b_main, w_small, b_small)
    gat3 = gat.reshape(B, S, LANES)
    logf_t = jnp.swapaxes(gat3[:, :, :A_HEADS], 1, 2).reshape(B * A_HEADS, S)
    f_rows = _cumsum_lanes(logf_t, 512, LOG2E).reshape(B, A_HEADS // 2, 2, S)
    o_a = _fox_prompt(q.reshape(B, S, A_WIDTH), kb.reshape(B, S, A_WIDTH), vb.reshape(B, S, A_WIDTH),
                      f_rows, ATTN_TQ, ATTN_TK)
    gat_t = jnp.swapaxes(gat3[:, :, G_IG:G_IG + 2 * M_HEADS], 1, 2)
    o_b, C_p, n_p, m_p = _mlstm(
        u.reshape(B, S, 2 * M_WIDTH), mv.reshape(B, S, M_WIDTH), mo.reshape(B, S, M_WIDTH), gat3, gat_t,
        jnp.zeros((B, M_HEADS, M_HEAD_DIM, M_HEAD_DIM), F32), jnp.zeros((B, M_HEADS, M_HEAD_DIM), F32),
        jnp.zeros((B, 1, LANES), F32), cw, cb, ng, PROMPT_CHUNK)
    xp = _merge(xp, o_a.reshape(B * S, A_WIDTH), o_b.reshape(B * S, M_WIDTH), gab, mod_p, tiles_pb,
                PROMPT_TM, w_a, w_b, w_o, lng[1], lnb[1])
    xp = _ffn(xp, mod_p, 6, tiles_pb, PROMPT_TM, wgu2, wd2, lng[2], lnb[2])

    prompt_state = (kf.reshape(1, B, S, A_HEADS, A_HEAD_DIM), vf.reshape(1, B, S, A_HEADS, A_HEAD_DIM),
                    gat3[None, :, :, :A_HEADS], C_p[None], n_p[None], m_p[None, :, 0, :M_HEADS],
                    u.reshape(B, S, 2 * M_WIDTH)[None, :, S - (CONV_W - 1):, :])

    xs = x_sample.reshape(rows_s, D)
    xs = _ffn(xs, mod_s, 0, 1, rows_s, wgu1, wd1, lng[0], lnb[0])
    q, kf, vf, kb, vb, gat, u, mv, mo, gab = _inproj(xs, mod_s, 1, rows_s, w_main, b_main, w_small, b_small)
    gat3 = gat.reshape(DB, DS, LANES)

    logf_all = jnp.concatenate([cache_fox_logf[l], gat3[:, :, :A_HEADS]], axis=1)
    total = P + LANES
    logf_all = jnp.pad(jnp.swapaxes(logf_all, 1, 2), ((0, 0), (0, 0), (0, total - P - DS)))
    f_all = _cumsum_lanes(logf_all.reshape(DB * A_HEADS, total), LANES, LOG2E).reshape(DB, A_HEADS, total)
    g_all = f_all - f_all[:, :, P - 1:P]
    pad_new = ((0, 0), (0, LANES - DS), (0, 0))
    o_a = _fox_sample(q.reshape(DB, DS, A_WIDTH),
                      cache_fox_k[l].reshape(DB, P, A_WIDTH), cache_fox_v[l].reshape(DB, P, A_WIDTH),
                      jnp.pad(kb.reshape(DB, DS, A_WIDTH), pad_new), jnp.pad(vb.reshape(DB, DS, A_WIDTH), pad_new),
                      g_all[:, :, :P], g_all[:, :, P:], DS)

    lead = SAMPLE_CHUNK - DS
    front = ((0, 0), (lead, 0), (0, 0))
    u3 = jnp.concatenate([jnp.zeros((DB, lead - (CONV_W - 1), 2 * M_WIDTH), F32), state_conv[l],
                          u.reshape(DB, DS, 2 * M_WIDTH)], axis=1)
    lane = jnp.arange(LANES)
    noop = jnp.where((lane >= G_IG) & (lane < G_LF), NEG_BIG, 0.0).astype(F32)
    gat_pad = jnp.concatenate([jnp.broadcast_to(noop, (DB, lead, LANES)), gat3], axis=1)
    gat_t = jnp.swapaxes(gat_pad[:, :, G_IG:G_IG + 2 * M_HEADS], 1, 2)
    m0 = jnp.pad(state_mlstm_m[l], ((0, 0), (0, LANES - M_HEADS)))[:, None, :]
    o_b, C_s, n_s, m_s = _mlstm(
        u3, jnp.pad(mv.reshape(DB, DS, M_WIDTH), front), jnp.pad(mo.reshape(DB, DS, M_WIDTH), front),
        gat_pad, gat_t, state_mlstm_C[l], state_mlstm_n[l], m0, cw, cb, ng, SAMPLE_CHUNK)
    xs = _merge(xs, o_a.reshape(rows_s, A_WIDTH), o_b[:, lead:, :].reshape(rows_s, M_WIDTH), gab, mod_s, 1,
                rows_s, w_a, w_b, w_o, lng[1], lnb[1])
    xs = _ffn(xs, mod_s, 6, 1, rows_s, wgu2, wd2, lng[2], lnb[2])

    sample_state = (kf.reshape(1, DB, DS, A_HEADS, A_HEAD_DIM), vf.reshape(1, DB, DS, A_HEADS, A_HEAD_DIM),
                    gat3[None, :, :, :A_HEADS], C_s[None], n_s[None], m_s[None, :, 0, :M_HEADS],
                    u3[None, :, SAMPLE_CHUNK - (CONV_W - 1):, :])

    return (xp.reshape(B, S, D), xs.reshape(DB, DS, D)) + prompt_state + sample_state
```

```python
import functools

import jax
import jax.numpy as jnp
import numpy as np
from jax import lax
from jax.experimental import pallas as pl
from jax.experimental.pallas import tpu as pltpu

F32 = jnp.float32
BF16 = jnp.bfloat16

D_MODEL = 1024
A_HEADS = 8
A_HEAD_DIM = 64
A_WIDTH = A_HEADS * A_HEAD_DIM
M_HEADS = 4
M_HEAD_DIM = 128
M_WIDTH = M_HEADS * M_HEAD_DIM
CONV_W = 4
D_FF = 2816
LN_EPS = 1e-5
DEPTH = 1
ALPHA = (2.0 * DEPTH) ** 0.25
LANES = 128
LOG2E = 1.4426950408889634
NEG_BIG = -1e30
VMEM_LIMIT = 56 * 1024 * 1024

MAIN_COLS = 3 * A_WIDTH + 4 * M_WIDTH + 2 * D_MODEL
G_IG = A_HEADS
G_LF = A_HEADS + M_HEADS


def _params(sem):
    return pltpu.CompilerParams(dimension_semantics=sem, vmem_limit_bytes=VMEM_LIMIT)


def _resident(shape):
    nd = len(shape)
    return pl.BlockSpec(shape, lambda *_: (0,) * nd, pipeline_mode=pl.Buffered(1))


def _sigmoid(x):
    return 1.0 / (1.0 + jnp.exp(-x))


def _layer_norm(y, g, b):
    mu = jnp.mean(y, axis=-1, keepdims=True)
    yc = y - mu
    var = jnp.mean(yc * yc, axis=-1, keepdims=True)
    return yc * lax.rsqrt(var + LN_EPS) * g + b


def _dot(a, b):
    return jnp.dot(a, b, preferred_element_type=F32)


def _dot_nt(a, b):
    return lax.dot_general(a, b, (((1,), (1,)), ((), ())), preferred_element_type=F32)


def _dot_tn(a, b):
    return lax.dot_general(a, b, (((0,), (0,)), ((), ())), preferred_element_type=F32)


def _dot_exact(a, b):
    return jnp.dot(a, b, preferred_element_type=F32, precision=lax.Precision.HIGHEST)


def _ada_kernel(c_ref, w_ref, b_ref, o_ref):
    c = c_ref[...]
    o_ref[...] = _dot(c * _sigmoid(c), w_ref[...]) + b_ref[...]


def _ada(c_all, w_ada, b_ada):
    rows = c_all.shape[0]
    n = w_ada.shape[1]
    tn = 1024
    return pl.pallas_call(
        _ada_kernel,
        grid=(n // tn,),
        in_specs=[pl.BlockSpec((rows, D_MODEL), lambda j: (0, 0)),
                  pl.BlockSpec((D_MODEL, tn), lambda j: (0, j)),
                  pl.BlockSpec((1, tn), lambda j: (0, j))],
        out_specs=pl.BlockSpec((rows, tn), lambda j: (0, j)),
        out_shape=jax.ShapeDtypeStruct((rows, n), F32),
        compiler_params=_params(("parallel",)),
        name="ada",
    )(c_all, w_ada, b_ada)


FFN_CHUNK = 256


def _ffn_kernel(x_ref, sh_ref, sc_ref, g_ref, wgu_ref, wd_ref, lng_ref, lnb_ref, o_ref):
    x = x_ref[...]
    h = (x * (1.0 + sc_ref[0]) + sh_ref[0]).astype(BF16)
    acc = jnp.zeros(x.shape, F32)
    for c in range(D_FF // FFN_CHUNK):
        lo = c * FFN_CHUNK
        gate = _dot(h, wgu_ref[:, lo:lo + FFN_CHUNK])
        up = _dot(h, wgu_ref[:, D_FF + lo:D_FF + lo + FFN_CHUNK])
        act = (gate * _sigmoid(gate) * up).astype(BF16)
        acc = acc + _dot(act, wd_ref[lo:lo + FFN_CHUNK, :])
    y = ALPHA * x + 0.5 * g_ref[0] * acc
    o_ref[...] = _layer_norm(y, lng_ref[...], lnb_ref[...])


def _mod_spec(mod, group, tiles_per_batch):
    return pl.BlockSpec((1, mod.shape[1], D_MODEL), lambda i: (i // tiles_per_batch, 0, group))


def _ffn(x, mod, group0, tiles_per_batch, tm, w_gu, w_down, ln_g, ln_b):
    rows = x.shape[0]
    row_spec = pl.BlockSpec((tm, D_MODEL), lambda i: (i, 0))
    return pl.pallas_call(
        _ffn_kernel,
        grid=(rows // tm,),
        in_specs=[row_spec,
                  _mod_spec(mod, group0, tiles_per_batch),
                  _mod_spec(mod, group0 + 1, tiles_per_batch),
                  _mod_spec(mod, group0 + 2, tiles_per_batch),
                  _resident(w_gu.shape), _resident(w_down.shape),
                  _resident(ln_g.shape), _resident(ln_b.shape)],
        out_specs=row_spec,
        out_shape=jax.ShapeDtypeStruct((rows, D_MODEL), F32),
        compiler_params=_params(("parallel",)),
        name="ffn",
    )(x, mod, mod, mod, w_gu, w_down, ln_g, ln_b)


def _inproj_kernel(x_ref, sh_ref, sc_ref, wm_ref, bm_ref, ws_ref, bs_ref,
                   q_ref, kf_ref, vf_ref, kb_ref, vb_ref, gat_ref, u_ref, mv_ref, mo_ref, gab_ref):
    x = x_ref[...]
    h = (x * (1.0 + sc_ref[0]) + sh_ref[0]).astype(BF16)

    def proj(lo, hi):
        return _dot(h, wm_ref[:, lo:hi]) + bm_ref[:, lo:hi]

    aw, mw = A_WIDTH, M_WIDTH
    q_ref[...] = (proj(0, aw) * (LOG2E * A_HEAD_DIM ** -0.5)).astype(BF16)
    k = proj(aw, 2 * aw)
    kf_ref[...] = k
    kb_ref[...] = k.astype(BF16)
    v = proj(2 * aw, 3 * aw)
    vf_ref[...] = v
    vb_ref[...] = v.astype(BF16)
    o = 3 * aw
    u_ref[...] = proj(o, o + 2 * mw)
    mv_ref[...] = proj(o + 2 * mw, o + 3 * mw).astype(BF16)
    mo_ref[...] = proj(o + 3 * mw, o + 4 * mw)
    gab_ref[...] = proj(o + 4 * mw, o + 4 * mw + 2 * D_MODEL)

    z = _dot(h, ws_ref[...]) + bs_ref[...]
    log_sig = jnp.minimum(z, 0.0) - jnp.log1p(jnp.exp(-jnp.abs(z)))
    lane = lax.broadcasted_iota(jnp.int32, z.shape, 1)
    is_input_gate = (lane >= G_IG) & (lane < G_LF)
    gat_ref[...] = jnp.where(is_input_gate, z, log_sig)


def _inproj(x, mod, tiles_per_batch, tm, w_main, b_main, w_small, b_small):
    rows = x.shape[0]

    def rs(width):
        return pl.BlockSpec((tm, width), lambda i: (i, 0))

    def out(width, dt):
        return jax.ShapeDtypeStruct((rows, width), dt)

    return pl.pallas_call(
        _inproj_kernel,
        grid=(rows // tm,),
        in_specs=[rs(D_MODEL),
                  _mod_spec(mod, 3, tiles_per_batch),
                  _mod_spec(mod, 4, tiles_per_batch),
                  _resident(w_main.shape), _resident(b_main.shape),
                  _resident(w_small.shape), _resident(b_small.shape)],
        out_specs=[rs(A_WIDTH), rs(A_WIDTH), rs(A_WIDTH), rs(A_WIDTH), rs(A_WIDTH),
                   rs(LANES), rs(2 * M_WIDTH), rs(M_WIDTH), rs(M_WIDTH), rs(2 * D_MODEL)],
        out_shape=[out(A_WIDTH, BF16), out(A_WIDTH, F32), out(A_WIDTH, F32),
                   out(A_WIDTH, BF16), out(A_WIDTH, BF16), out(LANES, F32),
                   out(2 * M_WIDTH, F32), out(M_WIDTH, BF16), out(M_WIDTH, F32),
                   out(2 * D_MODEL, F32)],
        compiler_params=_params(("parallel",)),
        name="inproj",
    )(x, mod, mod, w_main, b_main, w_small, b_small)


def _cumsum_kernel(x_ref, o_ref, carry_ref, *, scale):
    @pl.when(pl.program_id(0) == 0)
    def _():
        carry_ref[...] = jnp.zeros_like(carry_ref)

    t = x_ref.shape[1]
    upper = (lax.broadcasted_iota(jnp.int32, (t, t), 0)
             <= lax.broadcasted_iota(jnp.int32, (t, t), 1)).astype(F32)
    out = _dot_exact(x_ref[...], upper) + carry_ref[...]
    o_ref[...] = out * scale
    carry_ref[...] = out[:, t - 1:t]


def _cumsum_lanes(x, block, scale):
    rows, total = x.shape
    spec = pl.BlockSpec((rows, block), lambda j: (0, j))
    return pl.pallas_call(
        functools.partial(_cumsum_kernel, scale=scale),
        grid=(total // block,),
        in_specs=[spec],
        out_specs=spec,
        out_shape=jax.ShapeDtypeStruct((rows, total), F32),
        scratch_shapes=[pltpu.VMEM((rows, 1), F32)],
        compiler_params=_params(("arbitrary",)),
        name="cumsum",
    )(x)


def _cumsum_split_kernel(x_ref, hi_ref, mid_ref, lo_ref, carry_ref, *, scale):
    @pl.when(pl.program_id(0) == 0)
    def _():
        carry_ref[...] = jnp.zeros_like(carry_ref)

    t = x_ref.shape[1]
    upper = (lax.broadcasted_iota(jnp.int32, (t, t), 0)
             <= lax.broadcasted_iota(jnp.int32, (t, t), 1)).astype(F32)
    out = _dot_exact(x_ref[...], upper) + carry_ref[...]
    carry_ref[...] = out[:, t - 1:t]
    val = out * scale
    hi = val.astype(BF16)
    rest = val - hi.astype(F32)
    mid = rest.astype(BF16)
    hi_ref[...] = hi
    mid_ref[...] = mid
    lo_ref[...] = (rest - mid.astype(F32)).astype(BF16)


def _cumsum_lanes_split(x, block, scale):
    rows, total = x.shape
    spec = pl.BlockSpec((rows, block), lambda j: (0, j))
    piece = jax.ShapeDtypeStruct((rows, total), BF16)
    return pl.pallas_call(
        functools.partial(_cumsum_split_kernel, scale=scale),
        grid=(total // block,),
        in_specs=[spec],
        out_specs=[spec, spec, spec],
        out_shape=[piece, piece, piece],
        scratch_shapes=[pltpu.VMEM((rows, 1), F32)],
        compiler_params=_params(("arbitrary",)),
        name="cumsum_split",
    )(x)


HALF = LANES // 2
N_BIAS = 3
V_ONE = A_HEAD_DIM
V_ROWS = A_HEAD_DIM + 16


def _head_masks(q):
    lane = lax.broadcasted_iota(jnp.int32, q.shape, 1)
    zero = jnp.zeros_like(q)
    return jnp.where(lane < HALF, q, zero), jnp.where(lane >= HALF, q, zero)


def _fox_prompt_kernel(qt_ref, kx_ref, vt_ref, o_ref, sa_ref, sb_ref, m_ref, acc_ref, *, tq, tk):
    qi = pl.program_id(2)
    per_q = tq // tk
    heads = qt_ref.shape[1]
    m_ref[...] = jnp.full(m_ref.shape, NEG_BIG, F32)
    acc_ref[...] = jnp.zeros(acc_ref.shape, F32)

    bufs = (sa_ref, sb_ref)

    def scores(j, s_ref, lane0=0):
        start = pl.multiple_of(j * tk, tk)
        for g in range(heads):
            s_ref[g, :, lane0:] = _dot(kx_ref[0, g, pl.ds(start, tk), :], qt_ref[0, g, :, lane0:])

    def process(j, s_ref, lane0=0, diagonal=False):
        start = pl.multiple_of(j * tk, tk)

        def load(g):
            s = s_ref[g, :, lane0:]
            if diagonal:
                key = lax.broadcasted_iota(jnp.int32, s.shape, 0)
                qry = lax.broadcasted_iota(jnp.int32, s.shape, 1)
                s = jnp.where(key <= qry, s, NEG_BIG)
            return s

        for g in range(heads):
            m_old = m_ref[g, :, lane0:]
            m_new = jnp.maximum(m_old, jnp.max(load(g), axis=0, keepdims=True))
            p = jnp.exp2(load(g) - m_new).astype(BF16)
            pv = _dot(vt_ref[0, g, :, pl.ds(start, tk)], p)
            acc_ref[g, :, lane0:] = jnp.exp2(m_old - m_new) * acc_ref[g, :, lane0:] + pv
            m_ref[g, :, lane0:] = m_new

    def body(t, carry):
        j = t * per_q
        for u in range(per_q):
            scores(j + u + 1, bufs[(u + 1) % 2])
            process(j + u, bufs[u % 2])
        return carry

    first_diag = qi * per_q
    scores(0, sa_ref)
    lax.fori_loop(0, qi, body, 0)
    process(first_diag, sa_ref, diagonal=True)
    for d in range(1, per_q):
        scores(first_diag + d, bufs[d % 2], lane0=d * tk)
        process(first_diag + d, bufs[d % 2], lane0=d * tk, diagonal=True)

    for g in range(heads):
        acc = acc_ref[g]
        o_ref[0, g] = (acc[:A_HEAD_DIM] / acc[V_ONE:V_ONE + 1]).astype(o_ref.dtype)


def _fox_prompt(q, k, v, bias_pieces, tq, tk):
    b, s, h, dh = q.shape
    assert tq % (2 * tk) == 0 and tk % LANES == 0 and s % tq == 0
    ones = jnp.ones((b, h, N_BIAS, s), BF16)
    qt = jnp.concatenate([jnp.transpose(q, (0, 2, 3, 1)), ones,
                          jnp.zeros((b, h, LANES - dh - N_BIAS, s), BF16)], axis=2)
    kx = jnp.concatenate([jnp.transpose(k, (0, 2, 1, 3)), jnp.stack(bias_pieces, axis=-1),
                          jnp.zeros((b, h, s, LANES - dh - N_BIAS), BF16)], axis=3)
    vt = jnp.concatenate([jnp.transpose(v, (0, 2, 3, 1)), ones[:, :, :1],
                          jnp.zeros((b, h, V_ROWS - dh - 1, s), BF16)], axis=2)
    g = ATTN_HEADS_PER_STEP
    return pl.pallas_call(
        functools.partial(_fox_prompt_kernel, tq=tq, tk=tk),
        grid=(b, h // g, s // tq),
        in_specs=[pl.BlockSpec((1, g, LANES, tq), lambda bi, hi, i: (bi, hi, 0, i)),
                  pl.BlockSpec((1, g, s, LANES), lambda bi, hi, i: (bi, hi, 0, 0)),
                  pl.BlockSpec((1, g, V_ROWS, s), lambda bi, hi, i: (bi, hi, 0, 0))],
        out_specs=pl.BlockSpec((1, g, dh, tq), lambda bi, hi, i: (bi, hi, 0, i)),
        out_shape=jax.ShapeDtypeStruct((b, h, dh, s), BF16),
        scratch_shapes=[pltpu.VMEM((g, tk, tq), F32), pltpu.VMEM((g, tk, tq), F32),
                        pltpu.VMEM((g, 1, tq), F32), pltpu.VMEM((g, V_ROWS, tq), F32)],
        compiler_params=_params(("parallel", "parallel", "arbitrary")),
        name="fox_prompt",
    )(qt, kx, vt)


def _fox_sample_kernel(q_ref, kc_ref, vc_ref, kn_ref, vn_ref, gc_ref, gn_ref, o_ref, *, n_new):
    rows = q_ref.shape[1]
    pad = kn_ref.shape[1]
    row = lax.broadcasted_iota(jnp.int32, (rows, pad), 0)
    col = lax.broadcasted_iota(jnp.int32, (rows, pad), 1)
    visible = (col <= row) & (col < n_new)
    lane = lax.broadcasted_iota(jnp.int32, (rows, LANES), 1)
    for p in range(A_HEADS // 2):
        cols = slice(p * LANES, (p + 1) * LANES)
        q_heads = _head_masks(q_ref[0, :, cols])
        kc = kc_ref[0, :, cols].astype(BF16)
        vc = vc_ref[0, :, cols].astype(BF16)
        kn = kn_ref[0, :, cols]
        vn = vn_ref[0, :, cols]
        outs = []
        for hh in range(2):
            h = 2 * p + hh
            s_c = _dot_nt(q_heads[hh], kc) - gc_ref[0, h:h + 1, :]
            s_n = jnp.where(visible, _dot_nt(q_heads[hh], kn) - gn_ref[0, h:h + 1, :], NEG_BIG)
            m = jnp.maximum(jnp.max(s_c, axis=-1, keepdims=True), jnp.max(s_n, axis=-1, keepdims=True))
            p_c = jnp.exp2(s_c - m)
            p_n = jnp.exp2(s_n - m)
            l = jnp.sum(p_c, axis=-1, keepdims=True) + jnp.sum(p_n, axis=-1, keepdims=True)
            outs.append((_dot(p_c.astype(BF16), vc) + _dot(p_n.astype(BF16), vn)) / l)
        o_ref[0, :, cols] = jnp.where(lane < HALF, outs[0], outs[1]).astype(o_ref.dtype)


def _fox_sample(q, k_cache, v_cache, k_new, v_new, g_cache, g_new, n_new):
    b, l, w = q.shape
    p = k_cache.shape[1]
    pad = k_new.shape[1]

    def spec(r, c):
        return pl.BlockSpec((1, r, c), lambda i: (i, 0, 0))

    return pl.pallas_call(
        functools.partial(_fox_sample_kernel, n_new=n_new),
        grid=(b,),
        in_specs=[spec(l, w), spec(p, w), spec(p, w), spec(pad, w), spec(pad, w),
                  spec(A_HEADS, p), spec(A_HEADS, pad)],
        out_specs=spec(l, w),
        out_shape=jax.ShapeDtypeStruct((b, l, w), BF16),
        compiler_params=_params(("parallel",)),
        name="fox_sample",
    )(q, k_cache, v_cache, k_new, v_new, g_cache, g_new)


CONV_PAD = 8


def _mlstm_kernel(u_ref, mv_ref, mo_ref, gat_ref, gt_ref, c0_ref, n0_ref, m0_ref,
                  cw_ref, cb_ref, ng_ref,
                  ob_ref, c_ref, n_ref, m_ref, conv_ref):
    chunk = pl.program_id(1)
    L = u_ref.shape[1]
    hd = M_HEAD_DIM

    @pl.when(chunk == 0)
    def _():
        c_ref[...] = c0_ref[...]
        n_ref[...] = n0_ref[...]
        m_ref[...] = m0_ref[...]
        conv_ref[...] = jnp.zeros_like(conv_ref)

    conv_ref[CONV_PAD:CONV_PAD + L, :] = u_ref[0]
    y = cb_ref[...]
    for j in range(CONV_W):
        lo = CONV_PAD - (CONV_W - 1) + j
        y = y + conv_ref[lo:lo + L, :] * cw_ref[j:j + 1, :]
    tail = conv_ref[L + CONV_PAD - (CONV_W - 1):L + CONV_PAD, :]
    conv_ref[CONV_PAD - (CONV_W - 1):CONV_PAD, :] = tail
    qk = y * _sigmoid(y)

    gat = gat_ref[0]
    gt = gt_ref[0]
    r = lax.broadcasted_iota(jnp.int32, (L, L), 0)
    c = lax.broadcasted_iota(jnp.int32, (L, L), 1)
    causal = c <= r
    b_cols = _dot_exact(causal.astype(F32), gat)
    b_rows = _dot_exact(gt, (r <= c).astype(F32))
    m_prev_all = m_ref[0]
    lane = lax.broadcasted_iota(jnp.int32, (1, LANES), 1)
    m_next_all = jnp.zeros((1, LANES), F32)

    for h in range(M_HEADS):
        hs = slice(h * hd, (h + 1) * hd)
        q32 = qk[:, h * hd:(h + 1) * hd]
        k32 = qk[:, M_WIDTH + h * hd:M_WIDTH + (h + 1) * hd] * (hd ** -0.5)
        q = q32.astype(BF16)
        k = k32.astype(BF16)
        v = mv_ref[0, :, hs]
        bc = b_cols[:, G_LF + h:G_LF + h + 1]
        br = b_rows[M_HEADS + h:M_HEADS + h + 1, :]
        ir = gt[h:h + 1, :]
        ic = gat[:, G_IG + h:G_IG + h + 1]
        m_prev = m_prev_all[:, h:h + 1]
        c_state = c_ref[0, h]
        n_state = n_ref[0, h:h + 1, :]

        d = jnp.where(causal, bc - br + ir, NEG_BIG)
        inter = bc + m_prev
        m_t = jnp.maximum(inter, jnp.max(d, axis=-1, keepdims=True))
        w_inter = jnp.exp(inter - m_t)
        a = jnp.exp(d - m_t) * _dot_nt(q, k)
        num = _dot(a.astype(BF16), v) + w_inter * _dot_nt(q, c_state.astype(BF16))
        den = jnp.sum(a, axis=-1, keepdims=True) + w_inter * jnp.sum(q32 * n_state, axis=-1, keepdims=True)
        hval = num / jnp.maximum(jnp.abs(den), jnp.exp(-m_t))

        m_new = m_t[L - 1:L, :]
        b_last = bc[L - 1:L, :]
        w_state = jnp.exp(b_last + m_prev - m_new)
        w_s = jnp.exp(b_last - bc + ic - m_new)
        vw = (v.astype(F32) * w_s).astype(BF16)
        c_ref[0, h] = w_state * c_state + _dot_tn(vw, k)
        n_ref[0, h:h + 1, :] = w_state * n_state + jnp.sum(k32 * w_s, axis=0, keepdims=True)
        m_next_all = jnp.where(lane == h, m_new, m_next_all)

        mu = jnp.mean(hval, axis=-1, keepdims=True)
        hc = hval - mu
        var = jnp.mean(hc * hc, axis=-1, keepdims=True)
        hn = hc * lax.rsqrt(var + LN_EPS)
        ob_ref[0, :, hs] = (hn * ng_ref[:, hs] * _sigmoid(mo_ref[0, :, hs])).astype(ob_ref.dtype)

    m_ref[0] = m_next_all


def _mlstm(u, mv, mo, gat, gat_t, c0, n0, m0, conv_w, conv_b, norm_g, chunk):
    b, s, _ = u.shape

    def seq(width):
        return pl.BlockSpec((1, chunk, width), lambda bi, ci: (bi, ci, 0))

    def per_batch(shape):
        nd = len(shape)
        return pl.BlockSpec((1,) + tuple(shape[1:]), lambda bi, ci: (bi,) + (0,) * (nd - 1))

    return pl.pallas_call(
        _mlstm_kernel,
        grid=(b, s // chunk),
        in_specs=[seq(2 * M_WIDTH), seq(M_WIDTH), seq(M_WIDTH), seq(LANES),
                  pl.BlockSpec((1, 2 * M_HEADS, chunk), lambda bi, ci: (bi, 0, ci)),
                  per_batch(c0.shape), per_batch(n0.shape), per_batch(m0.shape),
                  _resident_2d(conv_w.shape), _resident_2d(conv_b.shape), _resident_2d(norm_g.shape)],
        out_specs=[seq(M_WIDTH), per_batch(c0.shape), per_batch(n0.shape), per_batch(m0.shape)],
        out_shape=[jax.ShapeDtypeStruct((b, s, M_WIDTH), BF16),
                   jax.ShapeDtypeStruct(c0.shape, F32),
                   jax.ShapeDtypeStruct(n0.shape, F32),
                   jax.ShapeDtypeStruct(m0.shape, F32)],
        scratch_shapes=[pltpu.VMEM((chunk + CONV_PAD, 2 * M_WIDTH), F32)],
        compiler_params=_params(("parallel", "arbitrary")),
        name="mlstm",
    )(u, mv, mo, gat, gat_t, c0, n0, m0, conv_w, conv_b, norm_g)


def _resident_2d(shape):
    return pl.BlockSpec(shape, lambda bi, ci: (0, 0))


def _merge_kernel(x_ref, oa_ref, ob_ref, gab_ref, g_ref, wa_ref, wb_ref, wo_ref, lng_ref, lnb_ref, o_ref):
    x = x_ref[...]
    ga = gab_ref[:, :D_MODEL]
    gb = gab_ref[:, D_MODEL:]
    m = _sigmoid(ga) * _dot(oa_ref[...], wa_ref[...]) + _sigmoid(gb) * _dot(ob_ref[...], wb_ref[...])
    mix = _dot(m.astype(BF16), wo_ref[...])
    y = ALPHA * x + g_ref[0] * mix
    o_ref[...] = _layer_norm(y, lng_ref[...], lnb_ref[...])


def _merge(x, oa, ob, gab, mod, tiles_per_batch, tm, w_a, w_b, w_out, ln_g, ln_b):
    rows = x.shape[0]

    def rs(width):
        return pl.BlockSpec((tm, width), lambda i: (i, 0))

    return pl.pallas_call(
        _merge_kernel,
        grid=(rows // tm,),
        in_specs=[rs(D_MODEL), rs(A_WIDTH), rs(M_WIDTH), rs(2 * D_MODEL),
                  _mod_spec(mod, 5, tiles_per_batch),
                  _resident(w_a.shape), _resident(w_b.shape), _resident(w_out.shape),
                  _resident(ln_g.shape), _resident(ln_b.shape)],
        out_specs=rs(D_MODEL),
        out_shape=jax.ShapeDtypeStruct((rows, D_MODEL), F32),
        compiler_params=_params(("parallel",)),
        name="merge",
    )(x, oa, ob, gab, mod, w_a, w_b, w_out, ln_g, ln_b)


PROMPT_TM = 512
ATTN_TQ = 1024
ATTN_TK = 256
ATTN_HEADS_PER_STEP = 2
PROMPT_CHUNK = 256
SAMPLE_CHUNK = 128


def _repack_in_proj(w_in, b_in):
    sizes = (A_WIDTH, A_WIDTH, A_WIDTH, A_HEADS, M_WIDTH, M_WIDTH, M_WIDTH, M_HEADS, M_HEADS, M_WIDTH,
             D_MODEL, D_MODEL)
    off = [int(o) for o in np.cumsum((0,) + sizes)]

    def cols(a, i):
        return a[..., off[i]:off[i + 1]]

    main = (0, 1, 2, 4, 5, 6, 9, 10, 11)
    small = (3, 7, 8)
    w_main = jnp.concatenate([cols(w_in, i) for i in main], axis=-1).astype(BF16)
    b_main = jnp.concatenate([cols(b_in, i) for i in main], axis=-1)[None, :]
    n_small = A_HEADS + 2 * M_HEADS
    w_small = jnp.pad(jnp.concatenate([cols(w_in, i) for i in small], axis=-1),
                      ((0, 0), (0, LANES - n_small))).astype(BF16)
    b_small = jnp.pad(jnp.concatenate([cols(b_in, i) for i in small], axis=-1), (0, LANES - n_small))[None, :]
    return w_main, b_main, w_small, b_small


def kernel(x_prompt, x_sample, cache_fox_k, cache_fox_v, cache_fox_logf, state_mlstm_C, state_mlstm_n, state_mlstm_m, state_conv, c_prompt, c_sample, w_ada, b_ada, ffn1_w_gu, ffn1_w_down, w_in, b_in, conv_w, conv_b, mlstm_norm_g, w_branch_a, w_branch_b, w_out, ffn2_w_gu, ffn2_w_down, ln_g, ln_b):
    assert w_ada.shape[0] == DEPTH == 1
    B, S, D = x_prompt.shape
    DB, DS, _ = x_sample.shape
    P = cache_fox_k.shape[2]
    l = 0

    w_main, b_main, w_small, b_small = _repack_in_proj(w_in[l], b_in[l])
    wgu1, wd1 = ffn1_w_gu[l].astype(BF16), ffn1_w_down[l].astype(BF16)
    wgu2, wd2 = ffn2_w_gu[l].astype(BF16), ffn2_w_down[l].astype(BF16)
    w_a, w_b, w_o = w_branch_a[l].astype(BF16), w_branch_b[l].astype(BF16), w_out[l].astype(BF16)
    lng = [ln_g[l, i][None, :] for i in range(3)]
    lnb = [ln_b[l, i][None, :] for i in range(3)]
    cw, cb, ng = conv_w[l], conv_b[l][None, :], mlstm_norm_g[l][None, :]

    n_c = B + DB
    c_rows = -(-n_c // 8) * 8
    c_all = jnp.pad(jnp.concatenate([c_prompt, c_sample], axis=0), ((0, c_rows - n_c), (0, 0)))
    mod = _ada(c_all, w_ada[l], b_ada[l][None, :])
    mod_p = mod[:B][:, None, :]
    mod_s = jnp.repeat(mod[B:n_c], DS, axis=0)[None]

    rows_s = DB * DS
    tiles_pb = S // PROMPT_TM

    xp = x_prompt.reshape(B * S, D)
    xp = _ffn(xp, mod_p, 0, tiles_pb, PROMPT_TM, wgu1, wd1, lng[0], lnb[0])
    q, kf, vf, kb, vb, gat, u, mv, mo, gab = _inproj(xp, mod_p, tiles_pb, PROMPT_TM,
                                                    w_main, b_main, w_small, b_small)
    gat3 = gat.reshape(B, S, LANES)
    logf_t = jnp.swapaxes(gat3[:, :, :A_HEADS], 1, 2).reshape(B * A_HEADS, S)
    bias_pieces = [a.reshape(B, A_HEADS, S) for a in _cumsum_lanes_split(logf_t, 512, -LOG2E)]
    head_shape = (B, S, A_HEADS, A_HEAD_DIM)
    o_a_t = _fox_prompt(q.reshape(head_shape), kb.reshape(head_shape), vb.reshape(head_shape),
                        bias_pieces, ATTN_TQ, ATTN_TK)
    o_a = jnp.transpose(o_a_t, (0, 3, 1, 2))
    gat_t = jnp.swapaxes(gat3[:, :, G_IG:G_IG + 2 * M_HEADS], 1, 2)
    o_b, C_p, n_p, m_p = _mlstm(
        u.reshape(B, S, 2 * M_WIDTH), mv.reshape(B, S, M_WIDTH), mo.reshape(B, S, M_WIDTH), gat3, gat_t,
        jnp.zeros((B, M_HEADS, M_HEAD_DIM, M_HEAD_DIM), F32), jnp.zeros((B, M_HEADS, M_HEAD_DIM), F32),
        jnp.zeros((B, 1, LANES), F32), cw, cb, ng, PROMPT_CHUNK)
    xp = _merge(xp, o_a.reshape(B * S, A_WIDTH), o_b.reshape(B * S, M_WIDTH), gab, mod_p, tiles_pb,
                PROMPT_TM, w_a, w_b, w_o, lng[1], lnb[1])
    xp = _ffn(xp, mod_p, 6, tiles_pb, PROMPT_TM, wgu2, wd2, lng[2], lnb[2])

    prompt_state = (kf.reshape(1, B, S, A_HEADS, A_HEAD_DIM), vf.reshape(1, B, S, A_HEADS, A_HEAD_DIM),
                    gat3[None, :, :, :A_HEADS], C_p[None], n_p[None], m_p[None, :, 0, :M_HEADS],
                    u.reshape(B, S, 2 * M_WIDTH)[None, :, S - (CONV_W - 1):, :])

    xs = x_sample.reshape(rows_s, D)
    xs = _ffn(xs, mod_s, 0, 1, rows_s, wgu1, wd1, lng[0], lnb[0])
    q, kf, vf, kb, vb, gat, u, mv, mo, gab = _inproj(xs, mod_s, 1, rows_s, w_main, b_main, w_small, b_small)
    gat3 = gat.reshape(DB, DS, LANES)

    logf_all = jnp.concatenate([cache_fox_logf[l], gat3[:, :, :A_HEADS]], axis=1)
    total = P + LANES
    logf_all = jnp.pad(jnp.swapaxes(logf_all, 1, 2), ((0, 0), (0, 0), (0, total - P - DS)))
    f_all = _cumsum_lanes(logf_all.reshape(DB * A_HEADS, total), LANES, LOG2E).reshape(DB, A_HEADS, total)
    g_all = f_all - f_all[:, :, P - 1:P]
    pad_new = ((0, 0), (0, LANES - DS), (0, 0))
    o_a = _fox_sample(q.reshape(DB, DS, A_WIDTH),
                      cache_fox_k[l].reshape(DB, P, A_WIDTH), cache_fox_v[l].reshape(DB, P, A_WIDTH),
                      jnp.pad(kb.reshape(DB, DS, A_WIDTH), pad_new), jnp.pad(vb.reshape(DB, DS, A_WIDTH), pad_new),
                      g_all[:, :, :P], g_all[:, :, P:], DS)

    lead = SAMPLE_CHUNK - DS
    front = ((0, 0), (lead, 0), (0, 0))
    u3 = jnp.concatenate([jnp.zeros((DB, lead - (CONV_W - 1), 2 * M_WIDTH), F32), state_conv[l],
                          u.reshape(DB, DS, 2 * M_WIDTH)], axis=1)
    lane = jnp.arange(LANES)
    noop = jnp.where((lane >= G_IG) & (lane < G_LF), NEG_BIG, 0.0).astype(F32)
    gat_pad = jnp.concatenate([jnp.broadcast_to(noop, (DB, lead, LANES)), gat3], axis=1)
    gat_t = jnp.swapaxes(gat_pad[:, :, G_IG:G_IG + 2 * M_HEADS], 1, 2)
    m0 = jnp.pad(state_mlstm_m[l], ((0, 0), (0, LANES - M_HEADS)))[:, None, :]
    o_b, C_s, n_s, m_s = _mlstm(
        u3, jnp.pad(mv.reshape(DB, DS, M_WIDTH), front), jnp.pad(mo.reshape(DB, DS, M_WIDTH), front),
        gat_pad, gat_t, state_mlstm_C[l], state_mlstm_n[l], m0, cw, cb, ng, SAMPLE_CHUNK)
    xs = _merge(xs, o_a.reshape(rows_s, A_WIDTH), o_b[:, lead:, :].reshape(rows_s, M_WIDTH), gab, mod_s, 1,
                rows_s, w_a, w_b, w_o, lng[1], lnb[1])
    xs = _ffn(xs, mod_s, 6, 1, rows_s, wgu2, wd2, lng[2], lnb[2])

    sample_state = (kf.reshape(1, DB, DS, A_HEADS, A_HEAD_DIM), vf.reshape(1, DB, DS, A_HEADS, A_HEAD_DIM),
                    gat3[None, :, :, :A_HEADS], C_s[None], n_s[None], m_s[None, :, 0, :M_HEADS],
                    u3[None, :, SAMPLE_CHUNK - (CONV_W - 1):, :])

    return (xp.reshape(B, S, D), xs.reshape(DB, DS, D)) + prompt_state + sample_state
```

```python
import functools

import jax
import jax.numpy as jnp
import numpy as np
from jax import lax
from jax.experimental import pallas as pl
from jax.experimental.pallas import tpu as pltpu

F32 = jnp.float32
BF16 = jnp.bfloat16

D_MODEL = 1024
A_HEADS = 8
A_HEAD_DIM = 64
A_WIDTH = A_HEADS * A_HEAD_DIM
M_HEADS = 4
M_HEAD_DIM = 128
M_WIDTH = M_HEADS * M_HEAD_DIM
CONV_W = 4
D_FF = 2816
LN_EPS = 1e-5
DEPTH = 1
ALPHA = (2.0 * DEPTH) ** 0.25
LANES = 128
LOG2E = 1.4426950408889634
NEG_BIG = -1e30
VMEM_LIMIT = 56 * 1024 * 1024

MAIN_COLS = 3 * A_WIDTH + 4 * M_WIDTH + 2 * D_MODEL
G_IG = A_HEADS
G_LF = A_HEADS + M_HEADS
PIECE_LANE = A_HEADS + 2 * M_HEADS


def _params(sem):
    return pltpu.CompilerParams(dimension_semantics=sem, vmem_limit_bytes=VMEM_LIMIT)


def _resident(shape):
    nd = len(shape)
    return pl.BlockSpec(shape, lambda *_: (0,) * nd, pipeline_mode=pl.Buffered(1))


def _sigmoid(x):
    return 1.0 / (1.0 + jnp.exp(-x))


def _layer_norm(y, g, b):
    mu = jnp.mean(y, axis=-1, keepdims=True)
    yc = y - mu
    var = jnp.mean(yc * yc, axis=-1, keepdims=True)
    return yc * lax.rsqrt(var + LN_EPS) * g + b


def _dot(a, b):
    return jnp.dot(a, b, preferred_element_type=F32)


def _dot_nt(a, b):
    return lax.dot_general(a, b, (((1,), (1,)), ((), ())), preferred_element_type=F32)


def _dot_tn(a, b):
    return lax.dot_general(a, b, (((0,), (0,)), ((), ())), preferred_element_type=F32)


def _dot_exact(a, b):
    return jnp.dot(a, b, preferred_element_type=F32, precision=lax.Precision.HIGHEST)


def _ada_kernel(c_ref, w_ref, b_ref, o_ref):
    c = c_ref[...]
    o_ref[...] = _dot(c * _sigmoid(c), w_ref[...]) + b_ref[...]


def _ada(c_all, w_ada, b_ada):
    rows = c_all.shape[0]
    n = w_ada.shape[1]
    tn = 1024
    return pl.pallas_call(
        _ada_kernel,
        grid=(n // tn,),
        in_specs=[pl.BlockSpec((rows, D_MODEL), lambda j: (0, 0)),
                  pl.BlockSpec((D_MODEL, tn), lambda j: (0, j)),
                  pl.BlockSpec((1, tn), lambda j: (0, j))],
        out_specs=pl.BlockSpec((rows, tn), lambda j: (0, j)),
        out_shape=jax.ShapeDtypeStruct((rows, n), F32),
        compiler_params=_params(("parallel",)),
        name="ada",
    )(c_all, w_ada, b_ada)


FFN_CHUNK = 256


def _ffn_kernel(x_ref, sh_ref, sc_ref, g_ref, wgu_ref, wd_ref, lng_ref, lnb_ref, o_ref):
    x = x_ref[...]
    h = (x * (1.0 + sc_ref[0]) + sh_ref[0]).astype(BF16)
    acc = jnp.zeros(x.shape, F32)
    for c in range(D_FF // FFN_CHUNK):
        lo = c * FFN_CHUNK
        gate = _dot(h, wgu_ref[:, lo:lo + FFN_CHUNK])
        up = _dot(h, wgu_ref[:, D_FF + lo:D_FF + lo + FFN_CHUNK])
        act = (gate * _sigmoid(gate) * up).astype(BF16)
        acc = acc + _dot(act, wd_ref[lo:lo + FFN_CHUNK, :])
    y = ALPHA * x + 0.5 * g_ref[0] * acc
    o_ref[...] = _layer_norm(y, lng_ref[...], lnb_ref[...])


def _mod_spec(mod, group, tiles_per_batch):
    return pl.BlockSpec((1, mod.shape[1], D_MODEL), lambda i: (i // tiles_per_batch, 0, group))


def _ffn(x, mod, group0, tiles_per_batch, tm, w_gu, w_down, ln_g, ln_b):
    rows = x.shape[0]
    row_spec = pl.BlockSpec((tm, D_MODEL), lambda i: (i, 0))
    return pl.pallas_call(
        _ffn_kernel,
        grid=(rows // tm,),
        in_specs=[row_spec,
                  _mod_spec(mod, group0, tiles_per_batch),
                  _mod_spec(mod, group0 + 1, tiles_per_batch),
                  _mod_spec(mod, group0 + 2, tiles_per_batch),
                  _resident(w_gu.shape), _resident(w_down.shape),
                  _resident(ln_g.shape), _resident(ln_b.shape)],
        out_specs=row_spec,
        out_shape=jax.ShapeDtypeStruct((rows, D_MODEL), F32),
        compiler_params=_params(("parallel",)),
        name="ffn",
    )(x, mod, mod, mod, w_gu, w_down, ln_g, ln_b)


def _inproj_kernel(x_ref, sh_ref, sc_ref, wm_ref, bm_ref, ws_ref, bs_ref, *rest, fox_transposed):
    if fox_transposed:
        (place_ref, qt_ref, kx_ref, vt_ref, kf_ref, vf_ref, gat_ref, u_ref, mv_ref, mo_ref, gab_ref,
         carry_ref) = rest
    else:
        q_ref, kb_ref, vb_ref, kf_ref, vf_ref, gat_ref, u_ref, mv_ref, mo_ref, gab_ref = rest
    x = x_ref[...]
    tm = x.shape[0]
    h = (x * (1.0 + sc_ref[0]) + sh_ref[0]).astype(BF16)

    def proj(lo, hi):
        return _dot(h, wm_ref[:, lo:hi]) + bm_ref[:, lo:hi]

    aw, mw = A_WIDTH, M_WIDTH
    q = proj(0, aw) * (LOG2E * A_HEAD_DIM ** -0.5)
    k = proj(aw, 2 * aw)
    v = proj(2 * aw, 3 * aw)
    kf_ref[...] = k
    vf_ref[...] = v
    o = 3 * aw
    u_ref[...] = proj(o, o + 2 * mw)
    mv_ref[...] = proj(o + 2 * mw, o + 3 * mw).astype(BF16)
    mo_ref[...] = proj(o + 3 * mw, o + 4 * mw)
    gab_ref[...] = proj(o + 4 * mw, o + 4 * mw + 2 * D_MODEL)

    z = _dot(h, ws_ref[...]) + bs_ref[...]
    log_sig = jnp.minimum(z, 0.0) - jnp.log1p(jnp.exp(-jnp.abs(z)))
    lane = lax.broadcasted_iota(jnp.int32, z.shape, 1)
    is_input_gate = (lane >= G_IG) & (lane < G_LF)
    gates = jnp.where(is_input_gate, z, log_sig)
    gat_ref[...] = gates

    if not fox_transposed:
        q_ref[...] = q.astype(BF16)
        kb_ref[...] = k.astype(BF16)
        vb_ref[...] = v.astype(BF16)
        return

    @pl.when(pl.program_id(1) == 0)
    def _():
        carry_ref[...] = jnp.zeros_like(carry_ref)

    row = lax.broadcasted_iota(jnp.int32, gates.shape, 0)
    cum = gates
    shift = 1
    while shift < tm:
        cum = cum + jnp.where(row >= shift, pltpu.roll(cum, shift, axis=0), 0.0)
        shift *= 2
    cum = cum + carry_ref[...]
    carry_ref[...] = cum[tm - 1:tm, :]
    val = cum * (-LOG2E)
    hi = val.astype(BF16)
    rest1 = val - hi.astype(F32)
    mid = rest1.astype(BF16)
    lo = (rest1 - mid.astype(F32)).astype(BF16)
    replica = lane // PIECE_LANE
    pieces = jnp.where(replica == 0, hi, jnp.where(replica == 1, mid, lo))
    placed = _dot(pieces, place_ref[...])

    dh = A_HEAD_DIM
    lane_g = lax.broadcasted_iota(jnp.int32, (tm, LANES), 1)
    qt = jnp.transpose(q)
    vt = jnp.transpose(v)
    row_q = lax.broadcasted_iota(jnp.int32, (LANES - dh, tm), 0)
    q_tail = jnp.where(row_q < N_BIAS, 1.0, 0.0).astype(BF16)
    row_v = lax.broadcasted_iota(jnp.int32, (V_ROWS - dh, tm), 0)
    v_tail = jnp.where(row_v < 1, 1.0, 0.0).astype(BF16)
    for hd in range(A_HEADS):
        pair = k[:, (hd // 2) * LANES:(hd // 2 + 1) * LANES]
        k_head = pair if hd % 2 == 0 else pltpu.roll(pair, HALF, axis=1)
        kx_ref[0, hd] = jnp.where(lane_g < dh, k_head, placed[:, hd * LANES:(hd + 1) * LANES]).astype(BF16)
        qt_ref[0, hd, :dh, :] = qt[hd * dh:(hd + 1) * dh, :].astype(BF16)
        qt_ref[0, hd, dh:, :] = q_tail
        vt_ref[0, hd, :dh, :] = vt[hd * dh:(hd + 1) * dh, :].astype(BF16)
        vt_ref[0, hd, dh:, :] = v_tail


def _bias_placement():
    place = np.zeros((LANES, A_HEADS * LANES), np.float32)
    for i in range(N_BIAS):
        for hd in range(A_HEADS):
            place[i * PIECE_LANE + hd, hd * LANES + A_HEAD_DIM + i] = 1.0
    return jnp.asarray(place, BF16)


def _inproj(x, mod, batches, tm, w_main, b_main, w_small, b_small, fox_transposed):
    rows = x.shape[0]
    s = rows // batches
    tiles = s // tm

    def rs(width):
        return pl.BlockSpec((tm, width), lambda b, i: (b * tiles + i, 0))

    def out(width, dt):
        return jax.ShapeDtypeStruct((rows, width), dt)

    def mod_spec(group):
        return pl.BlockSpec((1, mod.shape[1], D_MODEL), lambda b, i: (b, 0, group))

    def resident(a):
        return pl.BlockSpec(a.shape, lambda b, i: (0,) * a.ndim, pipeline_mode=pl.Buffered(1))

    operands = [x, mod, mod, w_main, b_main, w_small, b_small]
    in_specs = [rs(D_MODEL), mod_spec(3), mod_spec(4),
                resident(w_main), resident(b_main), resident(w_small), resident(b_small)]
    common_specs = [rs(A_WIDTH), rs(A_WIDTH), rs(LANES), rs(2 * M_WIDTH), rs(M_WIDTH), rs(M_WIDTH),
                    rs(2 * D_MODEL)]
    common_shapes = [out(A_WIDTH, F32), out(A_WIDTH, F32), out(LANES, F32), out(2 * M_WIDTH, F32),
                     out(M_WIDTH, BF16), out(M_WIDTH, F32), out(2 * D_MODEL, F32)]
    if fox_transposed:
        place = _bias_placement()
        operands.append(place)
        in_specs.append(resident(place))
        fox_specs = [pl.BlockSpec((1, A_HEADS, LANES, tm), lambda b, i: (b, 0, 0, i)),
                     pl.BlockSpec((1, A_HEADS, tm, LANES), lambda b, i: (b, 0, i, 0)),
                     pl.BlockSpec((1, A_HEADS, V_ROWS, tm), lambda b, i: (b, 0, 0, i))]
        fox_shapes = [jax.ShapeDtypeStruct((batches, A_HEADS, LANES, s), BF16),
                      jax.ShapeDtypeStruct((batches, A_HEADS, s, LANES), BF16),
                      jax.ShapeDtypeStruct((batches, A_HEADS, V_ROWS, s), BF16)]
        scratch = [pltpu.VMEM((1, LANES), F32)]
    else:
        fox_specs = [rs(A_WIDTH)] * 3
        fox_shapes = [out(A_WIDTH, BF16)] * 3
        scratch = []
    return pl.pallas_call(
        functools.partial(_inproj_kernel, fox_transposed=fox_transposed),
        grid=(batches, tiles),
        in_specs=in_specs,
        out_specs=fox_specs + common_specs,
        out_shape=fox_shapes + common_shapes,
        scratch_shapes=scratch,
        compiler_params=_params(("parallel", "arbitrary")),
        name="inproj",
    )(*operands)


def _cumsum_kernel(x_ref, o_ref, carry_ref, *, scale):
    @pl.when(pl.program_id(0) == 0)
    def _():
        carry_ref[...] = jnp.zeros_like(carry_ref)

    t = x_ref.shape[1]
    upper = (lax.broadcasted_iota(jnp.int32, (t, t), 0)
             <= lax.broadcasted_iota(jnp.int32, (t, t), 1)).astype(F32)
    out = _dot_exact(x_ref[...], upper) + carry_ref[...]
    o_ref[...] = out * scale
    carry_ref[...] = out[:, t - 1:t]


def _cumsum_lanes(x, block, scale):
    rows, total = x.shape
    spec = pl.BlockSpec((rows, block), lambda j: (0, j))
    return pl.pallas_call(
        functools.partial(_cumsum_kernel, scale=scale),
        grid=(total // block,),
        in_specs=[spec],
        out_specs=spec,
        out_shape=jax.ShapeDtypeStruct((rows, total), F32),
        scratch_shapes=[pltpu.VMEM((rows, 1), F32)],
        compiler_params=_params(("arbitrary",)),
        name="cumsum",
    )(x)


HALF = LANES // 2
N_BIAS = 3
V_ONE = A_HEAD_DIM
V_ROWS = A_HEAD_DIM + 16


def _head_masks(q):
    lane = lax.broadcasted_iota(jnp.int32, q.shape, 1)
    zero = jnp.zeros_like(q)
    return jnp.where(lane < HALF, q, zero), jnp.where(lane >= HALF, q, zero)


def _fox_prompt_kernel(qt_ref, kx_ref, vt_ref, o_ref, sa_ref, sb_ref, m_ref, acc_ref, *, tq, tk):
    qi = pl.program_id(2)
    per_q = tq // tk
    heads = qt_ref.shape[1]
    m_ref[...] = jnp.full(m_ref.shape, NEG_BIG, F32)
    acc_ref[...] = jnp.zeros(acc_ref.shape, F32)

    bufs = (sa_ref, sb_ref)

    def scores(j, s_ref, lane0=0):
        start = pl.multiple_of(j * tk, tk)
        for g in range(heads):
            s_ref[g, :, lane0:] = _dot(kx_ref[0, g, pl.ds(start, tk), :], qt_ref[0, g, :, lane0:])

    def process(j, s_ref, lane0=0, diagonal=False):
        start = pl.multiple_of(j * tk, tk)

        def load(g):
            s = s_ref[g, :, lane0:]
            if diagonal:
                key = lax.broadcasted_iota(jnp.int32, s.shape, 0)
                qry = lax.broadcasted_iota(jnp.int32, s.shape, 1)
                s = jnp.where(key <= qry, s, NEG_BIG)
            return s

        for g in range(heads):
            m_old = m_ref[g, :, lane0:]
            m_new = jnp.maximum(m_old, jnp.max(load(g), axis=0, keepdims=True))
            p = jnp.exp2(load(g) - m_new).astype(BF16)
            pv = _dot(vt_ref[0, g, :, pl.ds(start, tk)], p)
            acc_ref[g, :, lane0:] = jnp.exp2(m_old - m_new) * acc_ref[g, :, lane0:] + pv
            m_ref[g, :, lane0:] = m_new

    def body(t, carry):
        j = t * per_q
        for u in range(per_q):
            scores(j + u + 1, bufs[(u + 1) % 2])
            process(j + u, bufs[u % 2])
        return carry

    first_diag = qi * per_q
    scores(0, sa_ref)
    lax.fori_loop(0, qi, body, 0)
    process(first_diag, sa_ref, diagonal=True)
    for d in range(1, per_q):
        scores(first_diag + d, bufs[d % 2], lane0=d * tk)
        process(first_diag + d, bufs[d % 2], lane0=d * tk, diagonal=True)

    for g in range(heads):
        acc = acc_ref[g]
        o_ref[0, g] = (acc[:A_HEAD_DIM] / acc[V_ONE:V_ONE + 1]).astype(o_ref.dtype)


def _fox_prompt(qt, kx, vt, tq, tk):
    b, h, _, s = qt.shape
    dh = A_HEAD_DIM
    assert tq % (2 * tk) == 0 and tk % LANES == 0 and s % tq == 0
    g = ATTN_HEADS_PER_STEP
    return pl.pallas_call(
        functools.partial(_fox_prompt_kernel, tq=tq, tk=tk),
        grid=(b, h // g, s // tq),
        in_specs=[pl.BlockSpec((1, g, LANES, tq), lambda bi, hi, i: (bi, hi, 0, i)),
                  pl.BlockSpec((1, g, s, LANES), lambda bi, hi, i: (bi, hi, 0, 0)),
                  pl.BlockSpec((1, g, V_ROWS, s), lambda bi, hi, i: (bi, hi, 0, 0))],
        out_specs=pl.BlockSpec((1, g, dh, tq), lambda bi, hi, i: (bi, hi, 0, i)),
        out_shape=jax.ShapeDtypeStruct((b, h, dh, s), BF16),
        scratch_shapes=[pltpu.VMEM((g, tk, tq), F32), pltpu.VMEM((g, tk, tq), F32),
                        pltpu.VMEM((g, 1, tq), F32), pltpu.VMEM((g, V_ROWS, tq), F32)],
        compiler_params=_params(("parallel", "parallel", "arbitrary")),
        name="fox_prompt",
    )(qt, kx, vt)


def _fox_sample_kernel(q_ref, kc_ref, vc_ref, kn_ref, vn_ref, gc_ref, gn_ref, o_ref, *, n_new):
    rows = q_ref.shape[1]
    pad = kn_ref.shape[1]
    row = lax.broadcasted_iota(jnp.int32, (rows, pad), 0)
    col = lax.broadcasted_iota(jnp.int32, (rows, pad), 1)
    visible = (col <= row) & (col < n_new)
    lane = lax.broadcasted_iota(jnp.int32, (rows, LANES), 1)
    for p in range(A_HEADS // 2):
        cols = slice(p * LANES, (p + 1) * LANES)
        q_heads = _head_masks(q_ref[0, :, cols])
        kc = kc_ref[0, :, cols].astype(BF16)
        vc = vc_ref[0, :, cols].astype(BF16)
        kn = kn_ref[0, :, cols]
        vn = vn_ref[0, :, cols]
        outs = []
        for hh in range(2):
            h = 2 * p + hh
            s_c = _dot_nt(q_heads[hh], kc) - gc_ref[0, h:h + 1, :]
            s_n = jnp.where(visible, _dot_nt(q_heads[hh], kn) - gn_ref[0, h:h + 1, :], NEG_BIG)
            m = jnp.maximum(jnp.max(s_c, axis=-1, keepdims=True), jnp.max(s_n, axis=-1, keepdims=True))
            p_c = jnp.exp2(s_c - m)
            p_n = jnp.exp2(s_n - m)
            l = jnp.sum(p_c, axis=-1, keepdims=True) + jnp.sum(p_n, axis=-1, keepdims=True)
            outs.append((_dot(p_c.astype(BF16), vc) + _dot(p_n.astype(BF16), vn)) / l)
        o_ref[0, :, cols] = jnp.where(lane < HALF, outs[0], outs[1]).astype(o_ref.dtype)


def _fox_sample(q, k_cache, v_cache, k_new, v_new, g_cache, g_new, n_new):
    b, l, w = q.shape
    p = k_cache.shape[1]
    pad = k_new.shape[1]

    def spec(r, c):
        return pl.BlockSpec((1, r, c), lambda i: (i, 0, 0))

    return pl.pallas_call(
        functools.partial(_fox_sample_kernel, n_new=n_new),
        grid=(b,),
        in_specs=[spec(l, w), spec(p, w), spec(p, w), spec(pad, w), spec(pad, w),
                  spec(A_HEADS, p), spec(A_HEADS, pad)],
        out_specs=spec(l, w),
        out_shape=jax.ShapeDtypeStruct((b, l, w), BF16),
        compiler_params=_params(("parallel",)),
        name="fox_sample",
    )(q, k_cache, v_cache, k_new, v_new, g_cache, g_new)


CONV_PAD = 8


def _mlstm_kernel(u_ref, mv_ref, mo_ref, gat_ref, gt_ref, c0_ref, n0_ref, m0_ref,
                  cw_ref, cb_ref, ng_ref,
                  ob_ref, c_ref, n_ref, m_ref, conv_ref):
    chunk = pl.program_id(1)
    L = u_ref.shape[1]
    hd = M_HEAD_DIM

    @pl.when(chunk == 0)
    def _():
        c_ref[...] = c0_ref[...]
        n_ref[...] = n0_ref[...]
        m_ref[...] = m0_ref[...]
        conv_ref[...] = jnp.zeros_like(conv_ref)

    conv_ref[CONV_PAD:CONV_PAD + L, :] = u_ref[0]
    y = cb_ref[...]
    for j in range(CONV_W):
        lo = CONV_PAD - (CONV_W - 1) + j
        y = y + conv_ref[lo:lo + L, :] * cw_ref[j:j + 1, :]
    tail = conv_ref[L + CONV_PAD - (CONV_W - 1):L + CONV_PAD, :]
    conv_ref[CONV_PAD - (CONV_W - 1):CONV_PAD, :] = tail
    qk = y * _sigmoid(y)

    gat = gat_ref[0]
    gt = gt_ref[0]
    r = lax.broadcasted_iota(jnp.int32, (L, L), 0)
    c = lax.broadcasted_iota(jnp.int32, (L, L), 1)
    causal = c <= r
    b_cols = _dot_exact(causal.astype(F32), gat)
    b_rows = _dot_exact(gt, (r <= c).astype(F32))
    m_prev_all = m_ref[0]
    lane = lax.broadcasted_iota(jnp.int32, (1, LANES), 1)
    m_next_all = jnp.zeros((1, LANES), F32)

    for h in range(M_HEADS):
        hs = slice(h * hd, (h + 1) * hd)
        q32 = qk[:, h * hd:(h + 1) * hd]
        k32 = qk[:, M_WIDTH + h * hd:M_WIDTH + (h + 1) * hd] * (hd ** -0.5)
        q = q32.astype(BF16)
        k = k32.astype(BF16)
        v = mv_ref[0, :, hs]
        bc = b_cols[:, G_LF + h:G_LF + h + 1]
        br = b_rows[M_HEADS + h:M_HEADS + h + 1, :]
        ir = gt[h:h + 1, :]
        ic = gat[:, G_IG + h:G_IG + h + 1]
        m_prev = m_prev_all[:, h:h + 1]
        c_state = c_ref[0, h]
        n_state = n_ref[0, h:h + 1, :]

        d = jnp.where(causal, bc - br + ir, NEG_BIG)
        inter = bc + m_prev
        m_t = jnp.maximum(inter, jnp.max(d, axis=-1, keepdims=True))
        w_inter = jnp.exp(inter - m_t)
        a = jnp.exp(d - m_t) * _dot_nt(q, k)
        num = _dot(a.astype(BF16), v) + w_inter * _dot_nt(q, c_state.astype(BF16))
        den = jnp.sum(a, axis=-1, keepdims=True) + w_inter * jnp.sum(q32 * n_state, axis=-1, keepdims=True)
        hval = num / jnp.maximum(jnp.abs(den), jnp.exp(-m_t))

        m_new = m_t[L - 1:L, :]
        b_last = bc[L - 1:L, :]
        w_state = jnp.exp(b_last + m_prev - m_new)
        w_s = jnp.exp(b_last - bc + ic - m_new)
        vw = (v.astype(F32) * w_s).astype(BF16)
        c_ref[0, h] = w_state * c_state + _dot_tn(vw, k)
        n_ref[0, h:h + 1, :] = w_state * n_state + jnp.sum(k32 * w_s, axis=0, keepdims=True)
        m_next_all = jnp.where(lane == h, m_new, m_next_all)

        mu = jnp.mean(hval, axis=-1, keepdims=True)
        hc = hval - mu
        var = jnp.mean(hc * hc, axis=-1, keepdims=True)
        hn = hc * lax.rsqrt(var + LN_EPS)
        ob_ref[0, :, hs] = (hn * ng_ref[:, hs] * _sigmoid(mo_ref[0, :, hs])).astype(ob_ref.dtype)

    m_ref[0] = m_next_all


def _mlstm(u, mv, mo, gat, gat_t, c0, n0, m0, conv_w, conv_b, norm_g, chunk):
    b, s, _ = u.shape

    def seq(width):
        return pl.BlockSpec((1, chunk, width), lambda bi, ci: (bi, ci, 0))

    def per_batch(shape):
        nd = len(shape)
        return pl.BlockSpec((1,) + tuple(shape[1:]), lambda bi, ci: (bi,) + (0,) * (nd - 1))

    return pl.pallas_call(
        _mlstm_kernel,
        grid=(b, s // chunk),
        in_specs=[seq(2 * M_WIDTH), seq(M_WIDTH), seq(M_WIDTH), seq(LANES),
                  pl.BlockSpec((1, 2 * M_HEADS, chunk), lambda bi, ci: (bi, 0, ci)),
                  per_batch(c0.shape), per_batch(n0.shape), per_batch(m0.shape),
                  _resident_2d(conv_w.shape), _resident_2d(conv_b.shape), _resident_2d(norm_g.shape)],
        out_specs=[seq(M_WIDTH), per_batch(c0.shape), per_batch(n0.shape), per_batch(m0.shape)],
        out_shape=[jax.ShapeDtypeStruct((b, s, M_WIDTH), BF16),
                   jax.ShapeDtypeStruct(c0.shape, F32),
                   jax.ShapeDtypeStruct(n0.shape, F32),
                   jax.ShapeDtypeStruct(m0.shape, F32)],
        scratch_shapes=[pltpu.VMEM((chunk + CONV_PAD, 2 * M_WIDTH), F32)],
        compiler_params=_params(("parallel", "arbitrary")),
        name="mlstm",
    )(u, mv, mo, gat, gat_t, c0, n0, m0, conv_w, conv_b, norm_g)


def _resident_2d(shape):
    return pl.BlockSpec(shape, lambda bi, ci: (0, 0))


def _merge_kernel(x_ref, oa_ref, ob_ref, gab_ref, g_ref, wa_ref, wb_ref, wo_ref, lng_ref, lnb_ref, o_ref):
    x = x_ref[...]
    ga = gab_ref[:, :D_MODEL]
    gb = gab_ref[:, D_MODEL:]
    oa_t = oa_ref[0].reshape(A_WIDTH, x.shape[0])
    m = _sigmoid(ga) * _dot_tn(oa_t, wa_ref[...]) + _sigmoid(gb) * _dot(ob_ref[...], wb_ref[...])
    mix = _dot(m.astype(BF16), wo_ref[...])
    y = ALPHA * x + g_ref[0] * mix
    o_ref[...] = _layer_norm(y, lng_ref[...], lnb_ref[...])


def _merge(x, oa_t, ob, gab, mod, tiles_per_batch, tm, w_a, w_b, w_out, ln_g, ln_b):
    rows = x.shape[0]

    def rs(width):
        return pl.BlockSpec((tm, width), lambda i: (i, 0))

    oa_spec = pl.BlockSpec((1, A_HEADS, A_HEAD_DIM, tm),
                           lambda i: (i // tiles_per_batch, 0, 0, i % tiles_per_batch))
    return pl.pallas_call(
        _merge_kernel,
        grid=(rows // tm,),
        in_specs=[rs(D_MODEL), oa_spec, rs(M_WIDTH), rs(2 * D_MODEL),
                  _mod_spec(mod, 5, tiles_per_batch),
                  _resident(w_a.shape), _resident(w_b.shape), _resident(w_out.shape),
                  _resident(ln_g.shape), _resident(ln_b.shape)],
        out_specs=rs(D_MODEL),
        out_shape=jax.ShapeDtypeStruct((rows, D_MODEL), F32),
        compiler_params=_params(("parallel",)),
        name="merge",
    )(x, oa_t, ob, gab, mod, w_a, w_b, w_out, ln_g, ln_b)


PROMPT_TM = 512
ATTN_TQ = 1024
ATTN_TK = 256
ATTN_HEADS_PER_STEP = 2
PROMPT_CHUNK = 256
SAMPLE_CHUNK = 128


def _repack_in_proj(w_in, b_in):
    sizes = (A_WIDTH, A_WIDTH, A_WIDTH, A_HEADS, M_WIDTH, M_WIDTH, M_WIDTH, M_HEADS, M_HEADS, M_WIDTH,
             D_MODEL, D_MODEL)
    off = [int(o) for o in np.cumsum((0,) + sizes)]

    def cols(a, i):
        return a[..., off[i]:off[i + 1]]

    main = (0, 1, 2, 4, 5, 6, 9, 10, 11)
    w_main = jnp.concatenate([cols(w_in, i) for i in main], axis=-1).astype(BF16)
    b_main = jnp.concatenate([cols(b_in, i) for i in main], axis=-1)[None, :]

    def small(a):
        gap = jnp.zeros(a.shape[:-1] + (PIECE_LANE - A_HEADS,), a.dtype)
        parts = [cols(a, 3), cols(a, 7), cols(a, 8)]
        for _ in range(N_BIAS - 1):
            parts += [cols(a, 3), gap]
        used = PIECE_LANE * N_BIAS
        return jnp.concatenate(parts + [jnp.zeros(a.shape[:-1] + (LANES - used,), a.dtype)], axis=-1)

    return w_main, b_main, small(w_in).astype(BF16), small(b_in)[None, :]


def kernel(x_prompt, x_sample, cache_fox_k, cache_fox_v, cache_fox_logf, state_mlstm_C, state_mlstm_n, state_mlstm_m, state_conv, c_prompt, c_sample, w_ada, b_ada, ffn1_w_gu, ffn1_w_down, w_in, b_in, conv_w, conv_b, mlstm_norm_g, w_branch_a, w_branch_b, w_out, ffn2_w_gu, ffn2_w_down, ln_g, ln_b):
    assert w_ada.shape[0] == DEPTH == 1
    B, S, D = x_prompt.shape
    DB, DS, _ = x_sample.shape
    P = cache_fox_k.shape[2]
    l = 0

    w_main, b_main, w_small, b_small = _repack_in_proj(w_in[l], b_in[l])
    wgu1, wd1 = ffn1_w_gu[l].astype(BF16), ffn1_w_down[l].astype(BF16)
    wgu2, wd2 = ffn2_w_gu[l].astype(BF16), ffn2_w_down[l].astype(BF16)
    w_a, w_b, w_o = w_branch_a[l].astype(BF16), w_branch_b[l].astype(BF16), w_out[l].astype(BF16)
    lng = [ln_g[l, i][None, :] for i in range(3)]
    lnb = [ln_b[l, i][None, :] for i in range(3)]
    cw, cb, ng = conv_w[l], conv_b[l][None, :], mlstm_norm_g[l][None, :]

    n_c = B + DB
    c_rows = -(-n_c // 8) * 8
    c_all = jnp.pad(jnp.concatenate([c_prompt, c_sample], axis=0), ((0, c_rows - n_c), (0, 0)))
    mod = _ada(c_all, w_ada[l], b_ada[l][None, :])
    mod_p = mod[:B][:, None, :]
    mod_s = jnp.repeat(mod[B:n_c], DS, axis=0)[None]

    rows_s = DB * DS
    tiles_pb = S // PROMPT_TM

    xp = x_prompt.reshape(B * S, D)
    xp = _ffn(xp, mod_p, 0, tiles_pb, PROMPT_TM, wgu1, wd1, lng[0], lnb[0])
    qt, kx, vt, kf, vf, gat, u, mv, mo, gab = _inproj(xp, mod_p, B, PROMPT_TM,
                                                     w_main, b_main, w_small, b_small, True)
    gat3 = gat.reshape(B, S, LANES)
    o_a_t = _fox_prompt(qt, kx, vt, ATTN_TQ, ATTN_TK)
    gat_t = jnp.swapaxes(gat3[:, :, G_IG:G_IG + 2 * M_HEADS], 1, 2)
    o_b, C_p, n_p, m_p = _mlstm(
        u.reshape(B, S, 2 * M_WIDTH), mv.reshape(B, S, M_WIDTH), mo.reshape(B, S, M_WIDTH), gat3, gat_t,
        jnp.zeros((B, M_HEADS, M_HEAD_DIM, M_HEAD_DIM), F32), jnp.zeros((B, M_HEADS, M_HEAD_DIM), F32),
        jnp.zeros((B, 1, LANES), F32), cw, cb, ng, PROMPT_CHUNK)
    xp = _merge(xp, o_a_t, o_b.reshape(B * S, M_WIDTH), gab, mod_p, tiles_pb,
                PROMPT_TM, w_a, w_b, w_o, lng[1], lnb[1])
    xp = _ffn(xp, mod_p, 6, tiles_pb, PROMPT_TM, wgu2, wd2, lng[2], lnb[2])

    prompt_state = (kf.reshape(1, B, S, A_HEADS, A_HEAD_DIM), vf.reshape(1, B, S, A_HEADS, A_HEAD_DIM),
                    gat3[None, :, :, :A_HEADS], C_p[None], n_p[None], m_p[None, :, 0, :M_HEADS],
                    u.reshape(B, S, 2 * M_WIDTH)[None, :, S - (CONV_W - 1):, :])

    xs = x_sample.reshape(rows_s, D)
    xs = _ffn(xs, mod_s, 0, 1, rows_s, wgu1, wd1, lng[0], lnb[0])
    q, kb, vb, kf, vf, gat, u, mv, mo, gab = _inproj(xs, mod_s, 1, rows_s, w_main, b_main, w_small, b_small,
                                                    False)
    gat3 = gat.reshape(DB, DS, LANES)

    logf_all = jnp.concatenate([cache_fox_logf[l], gat3[:, :, :A_HEADS]], axis=1)
    total = P + LANES
    logf_all = jnp.pad(jnp.swapaxes(logf_all, 1, 2), ((0, 0), (0, 0), (0, total - P - DS)))
    f_all = _cumsum_lanes(logf_all.reshape(DB * A_HEADS, total), LANES, LOG2E).reshape(DB, A_HEADS, total)
    g_all = f_all - f_all[:, :, P - 1:P]
    pad_new = ((0, 0), (0, LANES - DS), (0, 0))
    o_a = _fox_sample(q.reshape(DB, DS, A_WIDTH),
                      cache_fox_k[l].reshape(DB, P, A_WIDTH), cache_fox_v[l].reshape(DB, P, A_WIDTH),
                      jnp.pad(kb.reshape(DB, DS, A_WIDTH), pad_new), jnp.pad(vb.reshape(DB, DS, A_WIDTH), pad_new),
                      g_all[:, :, :P], g_all[:, :, P:], DS)

    lead = SAMPLE_CHUNK - DS
    front = ((0, 0), (lead, 0), (0, 0))
    u3 = jnp.concatenate([jnp.zeros((DB, lead - (CONV_W - 1), 2 * M_WIDTH), F32), state_conv[l],
                          u.reshape(DB, DS, 2 * M_WIDTH)], axis=1)
    lane = jnp.arange(LANES)
    noop = jnp.where((lane >= G_IG) & (lane < G_LF), NEG_BIG, 0.0).astype(F32)
    gat_pad = jnp.concatenate([jnp.broadcast_to(noop, (DB, lead, LANES)), gat3], axis=1)
    gat_t = jnp.swapaxes(gat_pad[:, :, G_IG:G_IG + 2 * M_HEADS], 1, 2)
    m0 = jnp.pad(state_mlstm_m[l], ((0, 0), (0, LANES - M_HEADS)))[:, None, :]
    o_b, C_s, n_s, m_s = _mlstm(
        u3, jnp.pad(mv.reshape(DB, DS, M_WIDTH), front), jnp.pad(mo.reshape(DB, DS, M_WIDTH), front),
        gat_pad, gat_t, state_mlstm_C[l], state_mlstm_n[l], m0, cw, cb, ng, SAMPLE_CHUNK)
    o_a_t = jnp.transpose(o_a.reshape(rows_s, A_WIDTH)).reshape(1, A_HEADS, A_HEAD_DIM, rows_s)
    xs = _merge(xs, o_a_t, o_b[:, lead:, :].reshape(rows_s, M_WIDTH), gab, mod_s, 1,
                rows_s, w_a, w_b, w_o, lng[1], lnb[1])
    xs = _ffn(xs, mod_s, 6, 1, rows_s, wgu2, wd2, lng[2], lnb[2])

    sample_state = (kf.reshape(1, DB, DS, A_HEADS, A_HEAD_DIM), vf.reshape(1, DB, DS, A_HEADS, A_HEAD_DIM),
                    gat3[None, :, :, :A_HEADS], C_s[None], n_s[None], m_s[None, :, 0, :M_HEADS],
                    u3[None, :, SAMPLE_CHUNK - (CONV_W - 1):, :])

    return (xp.reshape(B, S, D), xs.reshape(DB, DS, D)) + prompt_state + sample_state
```

```python
import functools

import jax
import jax.numpy as jnp
import numpy as np
from jax import lax
from jax.experimental import pallas as pl
from jax.experimental.pallas import tpu as pltpu

F32 = jnp.float32
BF16 = jnp.bfloat16

D_MODEL = 1024
A_HEADS = 8
A_HEAD_DIM = 64
A_WIDTH = A_HEADS * A_HEAD_DIM
M_HEADS = 4
M_HEAD_DIM = 128
M_WIDTH = M_HEADS * M_HEAD_DIM
CONV_W = 4
D_FF = 2816
LN_EPS = 1e-5
DEPTH = 1
ALPHA = (2.0 * DEPTH) ** 0.25
LANES = 128
LOG2E = 1.4426950408889634
NEG_BIG = -1e30
VMEM_LIMIT = 56 * 1024 * 1024

MAIN_COLS = 3 * A_WIDTH + 4 * M_WIDTH + 2 * D_MODEL
G_IG = A_HEADS
G_LF = A_HEADS + M_HEADS
PIECE_LANE = A_HEADS + 2 * M_HEADS


def _params(sem):
    return pltpu.CompilerParams(dimension_semantics=sem, vmem_limit_bytes=VMEM_LIMIT)


def _resident(shape):
    nd = len(shape)
    return pl.BlockSpec(shape, lambda *_: (0,) * nd, pipeline_mode=pl.Buffered(1))


def _sigmoid(x):
    return 1.0 / (1.0 + jnp.exp(-x))


def _layer_norm(y, g, b):
    mu = jnp.mean(y, axis=-1, keepdims=True)
    yc = y - mu
    var = jnp.mean(yc * yc, axis=-1, keepdims=True)
    return yc * lax.rsqrt(var + LN_EPS) * g + b


def _dot(a, b):
    return jnp.dot(a, b, preferred_element_type=F32)


def _dot_nt(a, b):
    return lax.dot_general(a, b, (((1,), (1,)), ((), ())), preferred_element_type=F32)


def _dot_tn(a, b):
    return lax.dot_general(a, b, (((0,), (0,)), ((), ())), preferred_element_type=F32)


def _dot_exact(a, b):
    return jnp.dot(a, b, preferred_element_type=F32, precision=lax.Precision.HIGHEST)


def _ada_kernel(c_ref, w_ref, b_ref, o_ref):
    c = c_ref[...]
    o_ref[...] = _dot(c * _sigmoid(c), w_ref[...]) + b_ref[...]


def _ada(c_all, w_ada, b_ada):
    rows = c_all.shape[0]
    n = w_ada.shape[1]
    tn = 1024
    return pl.pallas_call(
        _ada_kernel,
        grid=(n // tn,),
        in_specs=[pl.BlockSpec((rows, D_MODEL), lambda j: (0, 0)),
                  pl.BlockSpec((D_MODEL, tn), lambda j: (0, j)),
                  pl.BlockSpec((1, tn), lambda j: (0, j))],
        out_specs=pl.BlockSpec((rows, tn), lambda j: (0, j)),
        out_shape=jax.ShapeDtypeStruct((rows, n), F32),
        compiler_params=_params(("parallel",)),
        name="ada",
    )(c_all, w_ada, b_ada)


FFN_CHUNK = 256


def _ffn_kernel(x_ref, sh_ref, sc_ref, g_ref, wgu_ref, wd_ref, lng_ref, lnb_ref, o_ref):
    x = x_ref[...]
    h = (x * (1.0 + sc_ref[0]) + sh_ref[0]).astype(BF16)
    acc = jnp.zeros(x.shape, F32)
    for c in range(D_FF // FFN_CHUNK):
        lo = c * FFN_CHUNK
        gate = _dot(h, wgu_ref[:, lo:lo + FFN_CHUNK])
        up = _dot(h, wgu_ref[:, D_FF + lo:D_FF + lo + FFN_CHUNK])
        act = (gate * _sigmoid(gate) * up).astype(BF16)
        acc = acc + _dot(act, wd_ref[lo:lo + FFN_CHUNK, :])
    y = ALPHA * x + 0.5 * g_ref[0] * acc
    o_ref[...] = _layer_norm(y, lng_ref[...], lnb_ref[...])


def _mod_spec(mod, group, tiles_per_batch):
    return pl.BlockSpec((1, mod.shape[1], D_MODEL), lambda i: (i // tiles_per_batch, 0, group))


def _ffn(x, mod, group0, tiles_per_batch, tm, w_gu, w_down, ln_g, ln_b):
    rows = x.shape[0]
    row_spec = pl.BlockSpec((tm, D_MODEL), lambda i: (i, 0))
    return pl.pallas_call(
        _ffn_kernel,
        grid=(rows // tm,),
        in_specs=[row_spec,
                  _mod_spec(mod, group0, tiles_per_batch),
                  _mod_spec(mod, group0 + 1, tiles_per_batch),
                  _mod_spec(mod, group0 + 2, tiles_per_batch),
                  _resident(w_gu.shape), _resident(w_down.shape),
                  _resident(ln_g.shape), _resident(ln_b.shape)],
        out_specs=row_spec,
        out_shape=jax.ShapeDtypeStruct((rows, D_MODEL), F32),
        compiler_params=_params(("parallel",)),
        name="ffn",
    )(x, mod, mod, mod, w_gu, w_down, ln_g, ln_b)


def _inproj_kernel(x_ref, sh_ref, sc_ref, wm_ref, bm_ref, ws_ref, bs_ref, *rest, fox_transposed):
    if fox_transposed:
        (place_ref, qt_ref, kx_ref, vt_ref, kf_ref, vf_ref, gat_ref, u_ref, mv_ref, mo_ref, gab_ref,
         carry_ref) = rest
    else:
        q_ref, kb_ref, vb_ref, kf_ref, vf_ref, gat_ref, u_ref, mv_ref, mo_ref, gab_ref = rest
    x = x_ref[...]
    tm = x.shape[0]
    h = (x * (1.0 + sc_ref[0]) + sh_ref[0]).astype(BF16)

    def proj(lo, hi):
        return _dot(h, wm_ref[:, lo:hi]) + bm_ref[:, lo:hi]

    aw, mw = A_WIDTH, M_WIDTH
    q = proj(0, aw) * (LOG2E * A_HEAD_DIM ** -0.5)
    k = proj(aw, 2 * aw)
    v = proj(2 * aw, 3 * aw)
    if not fox_transposed:
        kf_ref[...] = k
        vf_ref[...] = v
    o = 3 * aw
    u_ref[...] = proj(o, o + 2 * mw)
    mv_ref[...] = proj(o + 2 * mw, o + 3 * mw).astype(BF16)
    mo_ref[...] = proj(o + 3 * mw, o + 4 * mw)
    gab_ref[...] = proj(o + 4 * mw, o + 4 * mw + 2 * D_MODEL)

    z = _dot(h, ws_ref[...]) + bs_ref[...]
    log_sig = jnp.minimum(z, 0.0) - jnp.log1p(jnp.exp(-jnp.abs(z)))
    lane = lax.broadcasted_iota(jnp.int32, z.shape, 1)
    is_input_gate = (lane >= G_IG) & (lane < G_LF)
    gates = jnp.where(is_input_gate, z, log_sig)
    gat_ref[...] = gates

    if not fox_transposed:
        q_ref[...] = q.astype(BF16)
        kb_ref[...] = k.astype(BF16)
        vb_ref[...] = v.astype(BF16)
        return

    @pl.when(pl.program_id(1) == 0)
    def _():
        carry_ref[...] = jnp.zeros_like(carry_ref)

    row = lax.broadcasted_iota(jnp.int32, gates.shape, 0)
    cum = gates
    shift = 1
    while shift < tm:
        cum = cum + jnp.where(row >= shift, pltpu.roll(cum, shift, axis=0), 0.0)
        shift *= 2
    cum = cum + carry_ref[...]
    carry_ref[...] = cum[tm - 1:tm, :]
    val = cum * (-LOG2E)
    hi = val.astype(BF16)
    rest1 = val - hi.astype(F32)
    mid = rest1.astype(BF16)
    lo = (rest1 - mid.astype(F32)).astype(BF16)
    replica = lane // PIECE_LANE
    pieces = jnp.where(replica == 0, hi, jnp.where(replica == 1, mid, lo))
    placed = _dot(pieces, place_ref[...])

    dh = A_HEAD_DIM
    lane_g = lax.broadcasted_iota(jnp.int32, (tm, LANES), 1)
    qt = jnp.transpose(q)
    vt = jnp.transpose(v)
    row_q = lax.broadcasted_iota(jnp.int32, (LANES - dh, tm), 0)
    q_tail = jnp.where(row_q < N_BIAS, 1.0, 0.0).astype(BF16)
    row_v = lax.broadcasted_iota(jnp.int32, (V_ROWS - dh, tm), 0)
    v_tail = jnp.where(row_v < 1, 1.0, 0.0).astype(BF16)
    def head_lanes(a, hd):
        pair = a[:, (hd // 2) * LANES:(hd // 2 + 1) * LANES]
        return pair if hd % 2 == 0 else pltpu.roll(pair, HALF, axis=1)

    for hd in range(A_HEADS):
        k_head = head_lanes(k, hd)
        kf_ref[:, hd, :] = k_head[:, :dh]
        vf_ref[:, hd, :] = head_lanes(v, hd)[:, :dh]
        kx_ref[0, hd] = jnp.where(lane_g < dh, k_head, placed[:, hd * LANES:(hd + 1) * LANES]).astype(BF16)
        qt_ref[0, hd, :dh, :] = qt[hd * dh:(hd + 1) * dh, :].astype(BF16)
        qt_ref[0, hd, dh:, :] = q_tail
        vt_ref[0, hd, :dh, :] = vt[hd * dh:(hd + 1) * dh, :].astype(BF16)
        vt_ref[0, hd, dh:, :] = v_tail


def _bias_placement():
    place = np.zeros((LANES, A_HEADS * LANES), np.float32)
    for i in range(N_BIAS):
        for hd in range(A_HEADS):
            place[i * PIECE_LANE + hd, hd * LANES + A_HEAD_DIM + i] = 1.0
    return jnp.asarray(place, BF16)


def _inproj(x, mod, batches, tm, w_main, b_main, w_small, b_small, fox_transposed):
    rows = x.shape[0]
    s = rows // batches
    tiles = s // tm

    def rs(width):
        return pl.BlockSpec((tm, width), lambda b, i: (b * tiles + i, 0))

    def out(width, dt):
        return jax.ShapeDtypeStruct((rows, width), dt)

    def mod_spec(group):
        return pl.BlockSpec((1, mod.shape[1], D_MODEL), lambda b, i: (b, 0, group))

    def resident(a):
        return pl.BlockSpec(a.shape, lambda b, i: (0,) * a.ndim, pipeline_mode=pl.Buffered(1))

    operands = [x, mod, mod, w_main, b_main, w_small, b_small]
    in_specs = [rs(D_MODEL), mod_spec(3), mod_spec(4),
                resident(w_main), resident(b_main), resident(w_small), resident(b_small)]
    if fox_transposed:
        kv_spec = pl.BlockSpec((tm, A_HEADS, A_HEAD_DIM), lambda b, i: (b * tiles + i, 0, 0))
        kv_shape = jax.ShapeDtypeStruct((rows, A_HEADS, A_HEAD_DIM), F32)
    else:
        kv_spec, kv_shape = rs(A_WIDTH), out(A_WIDTH, F32)
    common_specs = [kv_spec, kv_spec, rs(LANES), rs(2 * M_WIDTH), rs(M_WIDTH), rs(M_WIDTH),
                    rs(2 * D_MODEL)]
    common_shapes = [kv_shape, kv_shape, out(LANES, F32), out(2 * M_WIDTH, F32),
                     out(M_WIDTH, BF16), out(M_WIDTH, F32), out(2 * D_MODEL, F32)]
    if fox_transposed:
        place = _bias_placement()
        operands.append(place)
        in_specs.append(resident(place))
        fox_specs = [pl.BlockSpec((1, A_HEADS, LANES, tm), lambda b, i: (b, 0, 0, i)),
                     pl.BlockSpec((1, A_HEADS, tm, LANES), lambda b, i: (b, 0, i, 0)),
                     pl.BlockSpec((1, A_HEADS, V_ROWS, tm), lambda b, i: (b, 0, 0, i))]
        fox_shapes = [jax.ShapeDtypeStruct((batches, A_HEADS, LANES, s), BF16),
                      jax.ShapeDtypeStruct((batches, A_HEADS, s, LANES), BF16),
                      jax.ShapeDtypeStruct((batches, A_HEADS, V_ROWS, s), BF16)]
        scratch = [pltpu.VMEM((1, LANES), F32)]
    else:
        fox_specs = [rs(A_WIDTH)] * 3
        fox_shapes = [out(A_WIDTH, BF16)] * 3
        scratch = []
    return pl.pallas_call(
        functools.partial(_inproj_kernel, fox_transposed=fox_transposed),
        grid=(batches, tiles),
        in_specs=in_specs,
        out_specs=fox_specs + common_specs,
        out_shape=fox_shapes + common_shapes,
        scratch_shapes=scratch,
        compiler_params=_params(("parallel", "arbitrary")),
        name="inproj",
    )(*operands)


def _cumsum_kernel(x_ref, o_ref, carry_ref, *, scale):
    @pl.when(pl.program_id(0) == 0)
    def _():
        carry_ref[...] = jnp.zeros_like(carry_ref)

    t = x_ref.shape[1]
    upper = (lax.broadcasted_iota(jnp.int32, (t, t), 0)
             <= lax.broadcasted_iota(jnp.int32, (t, t), 1)).astype(F32)
    out = _dot_exact(x_ref[...], upper) + carry_ref[...]
    o_ref[...] = out * scale
    carry_ref[...] = out[:, t - 1:t]


def _cumsum_lanes(x, block, scale):
    rows, total = x.shape
    spec = pl.BlockSpec((rows, block), lambda j: (0, j))
    return pl.pallas_call(
        functools.partial(_cumsum_kernel, scale=scale),
        grid=(total // block,),
        in_specs=[spec],
        out_specs=spec,
        out_shape=jax.ShapeDtypeStruct((rows, total), F32),
        scratch_shapes=[pltpu.VMEM((rows, 1), F32)],
        compiler_params=_params(("arbitrary",)),
        name="cumsum",
    )(x)


HALF = LANES // 2
N_BIAS = 3
V_ONE = A_HEAD_DIM
V_ROWS = A_HEAD_DIM + 16


def _head_masks(q):
    lane = lax.broadcasted_iota(jnp.int32, q.shape, 1)
    zero = jnp.zeros_like(q)
    return jnp.where(lane < HALF, q, zero), jnp.where(lane >= HALF, q, zero)


def _fox_prompt_kernel(qt_ref, kx_ref, vt_ref, o_ref, sa_ref, sb_ref, m_ref, acc_ref, *, tq, tk):
    qi = pl.program_id(2)
    per_q = tq // tk
    heads = qt_ref.shape[1]
    m_ref[...] = jnp.full(m_ref.shape, NEG_BIG, F32)
    acc_ref[...] = jnp.zeros(acc_ref.shape, F32)

    bufs = (sa_ref, sb_ref)

    def scores(j, s_ref, lane0=0):
        start = pl.multiple_of(j * tk, tk)
        for g in range(heads):
            s_ref[g, :, lane0:] = _dot(kx_ref[0, g, pl.ds(start, tk), :], qt_ref[0, g, :, lane0:])

    def process(j, s_ref, lane0=0, diagonal=False):
        start = pl.multiple_of(j * tk, tk)

        def load(g):
            s = s_ref[g, :, lane0:]
            if diagonal:
                key = lax.broadcasted_iota(jnp.int32, s.shape, 0)
                qry = lax.broadcasted_iota(jnp.int32, s.shape, 1)
                s = jnp.where(key <= qry, s, NEG_BIG)
            return s

        for g in range(heads):
            m_old = m_ref[g, :, lane0:]
            m_new = jnp.maximum(m_old, jnp.max(load(g), axis=0, keepdims=True))
            p = jnp.exp2(load(g) - m_new).astype(BF16)
            pv = _dot(vt_ref[0, g, :, pl.ds(start, tk)], p)
            acc_ref[g, :, lane0:] = jnp.exp2(m_old - m_new) * acc_ref[g, :, lane0:] + pv
            m_ref[g, :, lane0:] = m_new

    def body(t, carry):
        j = t * per_q
        for u in range(per_q):
            scores(j + u + 1, bufs[(u + 1) % 2])
            process(j + u, bufs[u % 2])
        return carry

    first_diag = qi * per_q
    scores(0, sa_ref)
    lax.fori_loop(0, qi, body, 0)
    for d in range(per_q):
        if d + 1 < per_q:
            scores(first_diag + d + 1, bufs[(d + 1) % 2], lane0=(d + 1) * tk)
        process(first_diag + d, bufs[d % 2], lane0=d * tk, diagonal=True)

    for g in range(heads):
        acc = acc_ref[g]
        o_ref[0, g] = (acc[:A_HEAD_DIM] / acc[V_ONE:V_ONE + 1]).astype(o_ref.dtype)


def _fox_prompt(qt, kx, vt, tq, tk):
    b, h, _, s = qt.shape
    dh = A_HEAD_DIM
    assert tq % (2 * tk) == 0 and tk % LANES == 0 and s % tq == 0
    g = ATTN_HEADS_PER_STEP
    return pl.pallas_call(
        functools.partial(_fox_prompt_kernel, tq=tq, tk=tk),
        grid=(b, h // g, s // tq),
        in_specs=[pl.BlockSpec((1, g, LANES, tq), lambda bi, hi, i: (bi, hi, 0, i)),
                  pl.BlockSpec((1, g, s, LANES), lambda bi, hi, i: (bi, hi, 0, 0)),
                  pl.BlockSpec((1, g, V_ROWS, s), lambda bi, hi, i: (bi, hi, 0, 0))],
        out_specs=pl.BlockSpec((1, g, dh, tq), lambda bi, hi, i: (bi, hi, 0, i)),
        out_shape=jax.ShapeDtypeStruct((b, h, dh, s), BF16),
        scratch_shapes=[pltpu.VMEM((g, tk, tq), F32), pltpu.VMEM((g, tk, tq), F32),
                        pltpu.VMEM((g, 1, tq), F32), pltpu.VMEM((g, V_ROWS, tq), F32)],
        compiler_params=_params(("parallel", "parallel", "arbitrary")),
        name="fox_prompt",
    )(qt, kx, vt)


def _fox_sample_kernel(q_ref, kc_ref, vc_ref, kn_ref, vn_ref, gc_ref, gn_ref, o_ref, *, n_new):
    rows = q_ref.shape[1]
    pad = kn_ref.shape[1]
    row = lax.broadcasted_iota(jnp.int32, (rows, pad), 0)
    col = lax.broadcasted_iota(jnp.int32, (rows, pad), 1)
    visible = (col <= row) & (col < n_new)
    lane = lax.broadcasted_iota(jnp.int32, (rows, LANES), 1)
    for p in range(A_HEADS // 2):
        cols = slice(p * LANES, (p + 1) * LANES)
        q_heads = _head_masks(q_ref[0, :, cols])
        kc = kc_ref[0, :, cols].astype(BF16)
        vc = vc_ref[0, :, cols].astype(BF16)
        kn = kn_ref[0, :, cols]
        vn = vn_ref[0, :, cols]
        outs = []
        for hh in range(2):
            h = 2 * p + hh
            s_c = _dot_nt(q_heads[hh], kc) - gc_ref[0, h:h + 1, :]
            s_n = jnp.where(visible, _dot_nt(q_heads[hh], kn) - gn_ref[0, h:h + 1, :], NEG_BIG)
            m = jnp.maximum(jnp.max(s_c, axis=-1, keepdims=True), jnp.max(s_n, axis=-1, keepdims=True))
            p_c = jnp.exp2(s_c - m)
            p_n = jnp.exp2(s_n - m)
            l = jnp.sum(p_c, axis=-1, keepdims=True) + jnp.sum(p_n, axis=-1, keepdims=True)
            outs.append((_dot(p_c.astype(BF16), vc) + _dot(p_n.astype(BF16), vn)) / l)
        o_ref[0, :, cols] = jnp.where(lane < HALF, outs[0], outs[1]).astype(o_ref.dtype)


def _fox_sample(q, k_cache, v_cache, k_new, v_new, g_cache, g_new, n_new):
    b, l, w = q.shape
    p = k_cache.shape[1]
    pad = k_new.shape[1]

    def spec(r, c):
        return pl.BlockSpec((1, r, c), lambda i: (i, 0, 0))

    return pl.pallas_call(
        functools.partial(_fox_sample_kernel, n_new=n_new),
        grid=(b,),
        in_specs=[spec(l, w), spec(p, w), spec(p, w), spec(pad, w), spec(pad, w),
                  spec(A_HEADS, p), spec(A_HEADS, pad)],
        out_specs=spec(l, w),
        out_shape=jax.ShapeDtypeStruct((b, l, w), BF16),
        compiler_params=_params(("parallel",)),
        name="fox_sample",
    )(q, k_cache, v_cache, k_new, v_new, g_cache, g_new)


CONV_PAD = 8


def _mlstm_kernel(u_ref, mv_ref, mo_ref, gat_ref, gt_ref, c0_ref, n0_ref, m0_ref,
                  cw_ref, cb_ref, ng_ref,
                  ob_ref, c_ref, n_ref, m_ref, conv_ref):
    chunk = pl.program_id(1)
    L = u_ref.shape[1]
    hd = M_HEAD_DIM

    @pl.when(chunk == 0)
    def _():
        c_ref[...] = c0_ref[...]
        n_ref[...] = n0_ref[...]
        m_ref[...] = m0_ref[...]
        conv_ref[...] = jnp.zeros_like(conv_ref)

    conv_ref[CONV_PAD:CONV_PAD + L, :] = u_ref[0]
    y = cb_ref[...]
    for j in range(CONV_W):
        lo = CONV_PAD - (CONV_W - 1) + j
        y = y + conv_ref[lo:lo + L, :] * cw_ref[j:j + 1, :]
    tail = conv_ref[L + CONV_PAD - (CONV_W - 1):L + CONV_PAD, :]
    conv_ref[CONV_PAD - (CONV_W - 1):CONV_PAD, :] = tail
    qk = y * _sigmoid(y)

    gat = gat_ref[0]
    gt = gt_ref[0]
    r = lax.broadcasted_iota(jnp.int32, (L, L), 0)
    c = lax.broadcasted_iota(jnp.int32, (L, L), 1)
    causal = c <= r
    b_cols = _dot_exact(causal.astype(F32), gat)
    b_rows = _dot_exact(gt, (r <= c).astype(F32))
    m_prev_all = m_ref[0]
    lane = lax.broadcasted_iota(jnp.int32, (1, LANES), 1)
    m_next_all = jnp.zeros((1, LANES), F32)

    for h in range(M_HEADS):
        hs = slice(h * hd, (h + 1) * hd)
        q32 = qk[:, h * hd:(h + 1) * hd]
        k32 = qk[:, M_WIDTH + h * hd:M_WIDTH + (h + 1) * hd] * (hd ** -0.5)
        q = q32.astype(BF16)
        k = k32.astype(BF16)
        v = mv_ref[0, :, hs]
        bc = b_cols[:, G_LF + h:G_LF + h + 1]
        br = b_rows[M_HEADS + h:M_HEADS + h + 1, :]
        ir = gt[h:h + 1, :]
        ic = gat[:, G_IG + h:G_IG + h + 1]
        m_prev = m_prev_all[:, h:h + 1]
        c_state = c_ref[0, h]
        n_state = n_ref[0, h:h + 1, :]

        d = jnp.where(causal, bc - br + ir, NEG_BIG)
        inter = bc + m_prev
        m_t = jnp.maximum(inter, jnp.max(d, axis=-1, keepdims=True))
        w_inter = jnp.exp(inter - m_t)
        a = jnp.exp(d - m_t) * _dot_nt(q, k)
        num = _dot(a.astype(BF16), v) + w_inter * _dot_nt(q, c_state.astype(BF16))
        den = jnp.sum(a, axis=-1, keepdims=True) + w_inter * jnp.sum(q32 * n_state, axis=-1, keepdims=True)
        hval = num / jnp.maximum(jnp.abs(den), jnp.exp(-m_t))

        m_new = m_t[L - 1:L, :]
        b_last = bc[L - 1:L, :]
        w_state = jnp.exp(b_last + m_prev - m_new)
        w_s = jnp.exp(b_last - bc + ic - m_new)
        vw = (v.astype(F32) * w_s).astype(BF16)
        c_ref[0, h] = w_state * c_state + _dot_tn(vw, k)
        n_ref[0, h:h + 1, :] = w_state * n_state + jnp.sum(k32 * w_s, axis=0, keepdims=True)
        m_next_all = jnp.where(lane == h, m_new, m_next_all)

        mu = jnp.mean(hval, axis=-1, keepdims=True)
        hc = hval - mu
        var = jnp.mean(hc * hc, axis=-1, keepdims=True)
        hn = hc * lax.rsqrt(var + LN_EPS)
        ob_ref[0, :, hs] = (hn * ng_ref[:, hs] * _sigmoid(mo_ref[0, :, hs])).astype(ob_ref.dtype)

    m_ref[0] = m_next_all


def _mlstm(u, mv, mo, gat, gat_t, c0, n0, m0, conv_w, conv_b, norm_g, chunk):
    b, s, _ = u.shape

    def seq(width):
        return pl.BlockSpec((1, chunk, width), lambda bi, ci: (bi, ci, 0))

    def per_batch(shape):
        nd = len(shape)
        return pl.BlockSpec((1,) + tuple(shape[1:]), lambda bi, ci: (bi,) + (0,) * (nd - 1))

    return pl.pallas_call(
        _mlstm_kernel,
        grid=(b, s // chunk),
        in_specs=[seq(2 * M_WIDTH), seq(M_WIDTH), seq(M_WIDTH), seq(LANES),
                  pl.BlockSpec((1, 2 * M_HEADS, chunk), lambda bi, ci: (bi, 0, ci)),
                  per_batch(c0.shape), per_batch(n0.shape), per_batch(m0.shape),
                  _resident_2d(conv_w.shape), _resident_2d(conv_b.shape), _resident_2d(norm_g.shape)],
        out_specs=[seq(M_WIDTH), per_batch(c0.shape), per_batch(n0.shape), per_batch(m0.shape)],
        out_shape=[jax.ShapeDtypeStruct((b, s, M_WIDTH), BF16),
                   jax.ShapeDtypeStruct(c0.shape, F32),
                   jax.ShapeDtypeStruct(n0.shape, F32),
                   jax.ShapeDtypeStruct(m0.shape, F32)],
        scratch_shapes=[pltpu.VMEM((chunk + CONV_PAD, 2 * M_WIDTH), F32)],
        compiler_params=_params(("parallel", "arbitrary")),
        name="mlstm",
    )(u, mv, mo, gat, gat_t, c0, n0, m0, conv_w, conv_b, norm_g)


def _resident_2d(shape):
    return pl.BlockSpec(shape, lambda bi, ci: (0, 0))


def _merge_kernel(x_ref, oa_ref, ob_ref, gab_ref, g_ref, wa_ref, wb_ref, wo_ref, lng_ref, lnb_ref, o_ref):
    x = x_ref[...]
    ga = gab_ref[:, :D_MODEL]
    gb = gab_ref[:, D_MODEL:]
    oa_t = oa_ref[0].reshape(A_WIDTH, x.shape[0])
    m = _sigmoid(ga) * _dot_tn(oa_t, wa_ref[...]) + _sigmoid(gb) * _dot(ob_ref[...], wb_ref[...])
    mix = _dot(m.astype(BF16), wo_ref[...])
    y = ALPHA * x + g_ref[0] * mix
    o_ref[...] = _layer_norm(y, lng_ref[...], lnb_ref[...])


def _merge(x, oa_t, ob, gab, mod, tiles_per_batch, tm, w_a, w_b, w_out, ln_g, ln_b):
    rows = x.shape[0]

    def rs(width):
        return pl.BlockSpec((tm, width), lambda i: (i, 0))

    oa_spec = pl.BlockSpec((1, A_HEADS, A_HEAD_DIM, tm),
                           lambda i: (i // tiles_per_batch, 0, 0, i % tiles_per_batch))
    return pl.pallas_call(
        _merge_kernel,
        grid=(rows // tm,),
        in_specs=[rs(D_MODEL), oa_spec, rs(M_WIDTH), rs(2 * D_MODEL),
                  _mod_spec(mod, 5, tiles_per_batch),
                  _resident(w_a.shape), _resident(w_b.shape), _resident(w_out.shape),
                  _resident(ln_g.shape), _resident(ln_b.shape)],
        out_specs=rs(D_MODEL),
        out_shape=jax.ShapeDtypeStruct((rows, D_MODEL), F32),
        compiler_params=_params(("parallel",)),
        name="merge",
    )(x, oa_t, ob, gab, mod, w_a, w_b, w_out, ln_g, ln_b)


PROMPT_TM = 512
MERGE_TM = 512
ATTN_TQ = 1024
ATTN_TK = 256
ATTN_HEADS_PER_STEP = 2
PROMPT_CHUNK = 256
SAMPLE_CHUNK = 128


def _repack_in_proj(w_in, b_in):
    sizes = (A_WIDTH, A_WIDTH, A_WIDTH, A_HEADS, M_WIDTH, M_WIDTH, M_WIDTH, M_HEADS, M_HEADS, M_WIDTH,
             D_MODEL, D_MODEL)
    off = [int(o) for o in np.cumsum((0,) + sizes)]

    def cols(a, i):
        return a[..., off[i]:off[i + 1]]

    main = (0, 1, 2, 4, 5, 6, 9, 10, 11)
    w_main = jnp.concatenate([cols(w_in, i) for i in main], axis=-1).astype(BF16)
    b_main = jnp.concatenate([cols(b_in, i) for i in main], axis=-1)[None, :]

    def small(a):
        gap = jnp.zeros(a.shape[:-1] + (PIECE_LANE - A_HEADS,), a.dtype)
        parts = [cols(a, 3), cols(a, 7), cols(a, 8)]
        for _ in range(N_BIAS - 1):
            parts += [cols(a, 3), gap]
        used = PIECE_LANE * N_BIAS
        return jnp.concatenate(parts + [jnp.zeros(a.shape[:-1] + (LANES - used,), a.dtype)], axis=-1)

    return w_main, b_main, small(w_in).astype(BF16), small(b_in)[None, :]


def kernel(x_prompt, x_sample, cache_fox_k, cache_fox_v, cache_fox_logf, state_mlstm_C, state_mlstm_n, state_mlstm_m, state_conv, c_prompt, c_sample, w_ada, b_ada, ffn1_w_gu, ffn1_w_down, w_in, b_in, conv_w, conv_b, mlstm_norm_g, w_branch_a, w_branch_b, w_out, ffn2_w_gu, ffn2_w_down, ln_g, ln_b):
    assert w_ada.shape[0] == DEPTH == 1
    B, S, D = x_prompt.shape
    DB, DS, _ = x_sample.shape
    P = cache_fox_k.shape[2]
    l = 0

    w_main, b_main, w_small, b_small = _repack_in_proj(w_in[l], b_in[l])
    wgu1, wd1 = ffn1_w_gu[l].astype(BF16), ffn1_w_down[l].astype(BF16)
    wgu2, wd2 = ffn2_w_gu[l].astype(BF16), ffn2_w_down[l].astype(BF16)
    w_a, w_b, w_o = w_branch_a[l].astype(BF16), w_branch_b[l].astype(BF16), w_out[l].astype(BF16)
    lng = [ln_g[l, i][None, :] for i in range(3)]
    lnb = [ln_b[l, i][None, :] for i in range(3)]
    cw, cb, ng = conv_w[l], conv_b[l][None, :], mlstm_norm_g[l][None, :]

    n_c = B + DB
    c_rows = -(-n_c // 8) * 8
    c_all = jnp.pad(jnp.concatenate([c_prompt, c_sample], axis=0), ((0, c_rows - n_c), (0, 0)))
    mod = _ada(c_all, w_ada[l], b_ada[l][None, :])
    mod_p = mod[:B][:, None, :]
    mod_s = jnp.repeat(mod[B:n_c], DS, axis=0)[None]

    rows_s = DB * DS
    tiles_pb = S // PROMPT_TM

    xp = x_prompt.reshape(B * S, D)
    xp = _ffn(xp, mod_p, 0, tiles_pb, PROMPT_TM, wgu1, wd1, lng[0], lnb[0])
    qt, kx, vt, kf, vf, gat, u, mv, mo, gab = _inproj(xp, mod_p, B, PROMPT_TM,
                                                     w_main, b_main, w_small, b_small, True)
    gat3 = gat.reshape(B, S, LANES)
    o_a_t = _fox_prompt(qt, kx, vt, ATTN_TQ, ATTN_TK)
    gat_t = jnp.swapaxes(gat3[:, :, G_IG:G_IG + 2 * M_HEADS], 1, 2)
    o_b, C_p, n_p, m_p = _mlstm(
        u.reshape(B, S, 2 * M_WIDTH), mv.reshape(B, S, M_WIDTH), mo.reshape(B, S, M_WIDTH), gat3, gat_t,
        jnp.zeros((B, M_HEADS, M_HEAD_DIM, M_HEAD_DIM), F32), jnp.zeros((B, M_HEADS, M_HEAD_DIM), F32),
        jnp.zeros((B, 1, LANES), F32), cw, cb, ng, PROMPT_CHUNK)
    xp = _merge(xp, o_a_t, o_b.reshape(B * S, M_WIDTH), gab, mod_p, S // MERGE_TM,
                MERGE_TM, w_a, w_b, w_o, lng[1], lnb[1])
    xp = _ffn(xp, mod_p, 6, tiles_pb, PROMPT_TM, wgu2, wd2, lng[2], lnb[2])

    prompt_state = (kf.reshape(1, B, S, A_HEADS, A_HEAD_DIM), vf.reshape(1, B, S, A_HEADS, A_HEAD_DIM),
                    gat3[None, :, :, :A_HEADS], C_p[None], n_p[None], m_p[None, :, 0, :M_HEADS],
                    u.reshape(B, S, 2 * M_WIDTH)[None, :, S - (CONV_W - 1):, :])

    xs = x_sample.reshape(rows_s, D)
    xs = _ffn(xs, mod_s, 0, 1, rows_s, wgu1, wd1, lng[0], lnb[0])
    q, kb, vb, kf, vf, gat, u, mv, mo, gab = _inproj(xs, mod_s, 1, rows_s, w_main, b_main, w_small, b_small,
                                                    False)
    gat3 = gat.reshape(DB, DS, LANES)

    logf_all = jnp.concatenate([cache_fox_logf[l], gat3[:, :, :A_HEADS]], axis=1)
    total = P + LANES
    logf_all = jnp.pad(jnp.swapaxes(logf_all, 1, 2), ((0, 0), (0, 0), (0, total - P - DS)))
    f_all = _cumsum_lanes(logf_all.reshape(DB * A_HEADS, total), LANES, LOG2E).reshape(DB, A_HEADS, total)
    g_all = f_all - f_all[:, :, P - 1:P]
    pad_new = ((0, 0), (0, LANES - DS), (0, 0))
    o_a = _fox_sample(q.reshape(DB, DS, A_WIDTH),
                      cache_fox_k[l].reshape(DB, P, A_WIDTH), cache_fox_v[l].reshape(DB, P, A_WIDTH),
                      jnp.pad(kb.reshape(DB, DS, A_WIDTH), pad_new), jnp.pad(vb.reshape(DB, DS, A_WIDTH), pad_new),
                      g_all[:, :, :P], g_all[:, :, P:], DS)

    lead = SAMPLE_CHUNK - DS
    front = ((0, 0), (lead, 0), (0, 0))
    u3 = jnp.concatenate([jnp.zeros((DB, lead - (CONV_W - 1), 2 * M_WIDTH), F32), state_conv[l],
                          u.reshape(DB, DS, 2 * M_WIDTH)], axis=1)
    lane = jnp.arange(LANES)
    noop = jnp.where((lane >= G_IG) & (lane < G_LF), NEG_BIG, 0.0).astype(F32)
    gat_pad = jnp.concatenate([jnp.broadcast_to(noop, (DB, lead, LANES)), gat3], axis=1)
    gat_t = jnp.swapaxes(gat_pad[:, :, G_IG:G_IG + 2 * M_HEADS], 1, 2)
    m0 = jnp.pad(state_mlstm_m[l], ((0, 0), (0, LANES - M_HEADS)))[:, None, :]
    o_b, C_s, n_s, m_s = _mlstm(
        u3, jnp.pad(mv.reshape(DB, DS, M_WIDTH), front), jnp.pad(mo.reshape(DB, DS, M_WIDTH), front),
        gat_pad, gat_t, state_mlstm_C[l], state_mlstm_n[l], m0, cw, cb, ng, SAMPLE_CHUNK)
    o_a_t = jnp.transpose(o_a.reshape(rows_s, A_WIDTH)).reshape(1, A_HEADS, A_HEAD_DIM, rows_s)
    xs = _merge(xs, o_a_t, o_b[:, lead:, :].reshape(rows_s, M_WIDTH), gab, mod_s, 1,
                rows_s, w_a, w_b, w_o, lng[1], lnb[1])
    xs = _ffn(xs, mod_s, 6, 1, rows_s, wgu2, wd2, lng[2], lnb[2])

    sample_state = (kf.reshape(1, DB, DS, A_HEADS, A_HEAD_DIM), vf.reshape(1, DB, DS, A_HEADS, A_HEAD_DIM),
                    gat3[None, :, :, :A_HEADS], C_s[None], n_s[None], m_s[None, :, 0, :M_HEADS],
                    u3[None, :, SAMPLE_CHUNK - (CONV_W - 1):, :])

    return (xp.reshape(B, S, D), xs.reshape(DB, DS, D)) + prompt_state + sample_state
```
